```python
import math
import jax, jax.numpy as jnp
from jax import lax
import numpy as np

D_MODEL = 1024
BATCH = 8
SEQ = 2048
DEPTH = 1

CHUNK = 64
Q_BLOCK = 128
HEAD_DIM = 64
N_DIFF_HEADS = 4
N_SB_HEADS = 8
DIFF_WIDTH = N_DIFF_HEADS * 2 * HEAD_DIM
SB_WIDTH = N_SB_HEADS * HEAD_DIM
MIX_WIDTH = DIFF_WIDTH + SB_WIDTH
IN_WIDTH = 3 * MIX_WIDTH
D_FF = 4 * D_MODEL
ROPE_THETA = 10000.0
NORM_EPS = 1e-6
N_MOD = 6

kernel_name = "hybrid_diff_stickbreaking_block"


def rms_norm(x, g):
    xf = x.astype(jnp.float32)
    y = xf * lax.rsqrt(jnp.mean(xf * xf, axis=-1, keepdims=True) + NORM_EPS)
    return (y * g.astype(jnp.float32)).astype(x.dtype)


def rope_tables(seq_len, dim):
    inv = 1.0 / (ROPE_THETA ** (jnp.arange(0, dim, 2, dtype=jnp.float32) / dim))
    ang = jnp.arange(seq_len, dtype=jnp.float32)[:, None] * inv[None, :]
    ang = jnp.concatenate([ang, ang], axis=-1)
    return jnp.cos(ang), jnp.sin(ang)


def apply_rope(x, cos, sin):
    x1, x2 = jnp.split(x, 2, axis=-1)
    rot = jnp.concatenate([-x2, x1], axis=-1)
    return (x * cos + rot * sin).astype(x.dtype)


def diff_attention(q, k, v, lam, subln_g, lambda_init):
    seq_len = q.shape[3]
    scale = HEAD_DIM ** -0.5
    outs = []
    for blk in range(seq_len // Q_BLOCK):
        q0 = blk * Q_BLOCK
        kv_len = q0 + Q_BLOCK
        qb = q[:, :, :, q0:kv_len]
        kb = k[:, :, :, :kv_len]
        vb = v[:, :, :kv_len]
        s = jnp.einsum("bhmqd,bhmkd->bhmqk", qb, kb).astype(jnp.float32) * scale
        q_chunk = (q0 + jnp.arange(Q_BLOCK)) // CHUNK
        k_chunk = jnp.arange(kv_len) // CHUNK
        mask = k_chunk[None, :] <= q_chunk[:, None]
        p = jax.nn.softmax(jnp.where(mask, s, -jnp.inf), axis=-1)
        w = p[:, :, 0] - lam * p[:, :, 1]
        outs.append(jnp.einsum("bhqk,bhkd->bhqd", w.astype(vb.dtype), vb))
    o = jnp.concatenate(outs, axis=2)
    return rms_norm(o, subln_g) * (1.0 - lambda_init)


def stick_breaking_attention(q, k, v):
    seq_len = q.shape[2]
    scale = HEAD_DIM ** -0.5
    outs = []
    for blk in range(seq_len // Q_BLOCK):
        q0 = blk * Q_BLOCK
        kv_len = q0 + Q_BLOCK
        qb = q[:, :, q0:kv_len]
        kb = k[:, :, :kv_len]
        vb = v[:, :, :kv_len]
        z = jnp.einsum("bhqd,bhkd->bhqk", qb, kb).astype(jnp.float32) * scale
        t_idx = q0 + jnp.arange(Q_BLOCK)
        s_idx = jnp.arange(kv_len)
        causal = s_idx[None, :] < t_idx[:, None]
        log_beta = jax.nn.log_sigmoid(z)
        log_1m = jnp.where(causal, jax.nn.log_sigmoid(-z), 0.0)
        suffix = lax.cumsum(log_1m, axis=3, reverse=True) - log_1m
        a = jnp.where(causal, jnp.exp(log_beta + suffix), 0.0)
        outs.append(jnp.einsum("bhqk,bhkd->bhqd", a.astype(vb.dtype), vb))
    return jnp.concatenate(outs, axis=2)


def setup_inputs(seed: int = 0) -> dict:
    key = jax.random.key(seed)
    ks = jax.random.split(key, 16)
    f32 = jnp.float32
    x = jax.random.normal(ks[0], (BATCH, SEQ, D_MODEL), f32)
    c = jax.random.normal(ks[1], (BATCH, D_MODEL), f32)
    ada_w = jax.random.normal(ks[2], (DEPTH, D_MODEL, N_MOD * D_MODEL), f32) * (0.5 * D_MODEL ** -0.5)
    ada_b = jax.random.normal(ks[3], (DEPTH, N_MOD * D_MODEL), f32) * 0.01
    mix_norm_g = 1.0 + 0.02 * jax.random.normal(ks[4], (DEPTH, D_MODEL), f32)
    w_in = jax.random.normal(ks[5], (DEPTH, D_MODEL, IN_WIDTH), f32) * D_MODEL ** -0.5
    lambda_q1 = 0.1 * jax.random.normal(ks[6], (DEPTH, HEAD_DIM), f32)
    lambda_k1 = 0.1 * jax.random.normal(ks[7], (DEPTH, HEAD_DIM), f32)
    lambda_q2 = 0.1 * jax.random.normal(ks[8], (DEPTH, HEAD_DIM), f32)
    lambda_k2 = 0.1 * jax.random.normal(ks[9], (DEPTH, HEAD_DIM), f32)
    diff_subln_g = 1.0 + 0.02 * jax.random.normal(ks[10], (DEPTH, 2 * HEAD_DIM), f32)
    w_out = jax.random.normal(ks[11], (DEPTH, MIX_WIDTH, D_MODEL), f32) * MIX_WIDTH ** -0.5
    ffn_norm_g = 1.0 + 0.02 * jax.random.normal(ks[12], (DEPTH, D_MODEL), f32)
    w_ff1 = jax.random.normal(ks[13], (DEPTH, D_MODEL, D_FF), f32) * D_MODEL ** -0.5
    w_ff2 = jax.random.normal(ks[14], (DEPTH, D_FF, D_MODEL), f32) * D_FF ** -0.5
    final_norm_g = 1.0 + 0.02 * jax.random.normal(ks[15], (D_MODEL,), f32)
    return {"x": x, "c": c, "ada_w": ada_w, "ada_b": ada_b, "mix_norm_g": mix_norm_g,
            "w_in": w_in, "lambda_q1": lambda_q1, "lambda_k1": lambda_k1,
            "lambda_q2": lambda_q2, "lambda_k2": lambda_k2, "diff_subln_g": diff_subln_g,
            "w_out": w_out, "ffn_norm_g": ffn_norm_g, "w_ff1": w_ff1, "w_ff2": w_ff2,
            "final_norm_g": final_norm_g}


def reference(x, c, ada_w, ada_b, mix_norm_g, w_in, lambda_q1, lambda_k1, lambda_q2,
              lambda_k2, diff_subln_g, w_out, ffn_norm_g, w_ff1, w_ff2, final_norm_g):
    bsz, seq_len, _ = x.shape
    cos, sin = rope_tables(seq_len, HEAD_DIM)
    cos = cos.astype(x.dtype)
    sin = sin.astype(x.dtype)
    c_act = jax.nn.silu(c)
    for layer in range(DEPTH):
        lambda_init = 0.8 - 0.6 * math.exp(-0.3 * layer)
        mod = c_act @ ada_w[layer] + ada_b[layer]
        sh_m, sc_m, g_m, sh_f, sc_f, g_f = [m[:, None, :] for m in jnp.split(mod, N_MOD, axis=-1)]

        h = rms_norm(x, mix_norm_g[layer]) * (1.0 + sc_m) + sh_m
        proj = h @ w_in[layer]
        dq, dk, dv, sq, sk, sv = jnp.split(proj, 6, axis=-1)
        dq = dq.reshape(bsz, seq_len, N_DIFF_HEADS, 2, HEAD_DIM).transpose(0, 2, 3, 1, 4)
        dk = dk.reshape(bsz, seq_len, N_DIFF_HEADS, 2, HEAD_DIM).transpose(0, 2, 3, 1, 4)
        dv = dv.reshape(bsz, seq_len, N_DIFF_HEADS, 2 * HEAD_DIM).transpose(0, 2, 1, 3)
        dq = apply_rope(dq, cos, sin)
        dk = apply_rope(dk, cos, sin)
        lam = (jnp.exp(jnp.sum(lambda_q1[layer].astype(jnp.float32) * lambda_k1[layer].astype(jnp.float32)))
               - jnp.exp(jnp.sum(lambda_q2[layer].astype(jnp.float32) * lambda_k2[layer].astype(jnp.float32)))
               + lambda_init)
        o_diff = diff_attention(dq, dk, dv, lam, diff_subln_g[layer], lambda_init)
        o_diff = o_diff.transpose(0, 2, 1, 3).reshape(bsz, seq_len, DIFF_WIDTH)

        sq = sq.reshape(bsz, seq_len, N_SB_HEADS, HEAD_DIM).transpose(0, 2, 1, 3)
        sk = sk.reshape(bsz, seq_len, N_SB_HEADS, HEAD_DIM).transpose(0, 2, 1, 3)
        sv = sv.reshape(bsz, seq_len, N_SB_HEADS, HEAD_DIM).transpose(0, 2, 1, 3)
        o_sb = stick_breaking_attention(sq, sk, sv)
        o_sb = o_sb.transpose(0, 2, 1, 3).reshape(bsz, seq_len, SB_WIDTH)

        mixed = jnp.concatenate([o_diff, o_sb], axis=-1) @ w_out[layer]
        x = x + g_m * mixed

        h = rms_norm(x, ffn_norm_g[layer]) * (1.0 + sc_f) + sh_f
        f = jnp.square(jax.nn.relu(h @ w_ff1[layer])) @ w_ff2[layer]
        x = x + g_f * f
    return rms_norm(x, final_norm_g)
```

```python
import functools
import math

import jax
import jax.numpy as jnp
from jax import lax
from jax.experimental import pallas as pl
from jax.experimental.pallas import tpu as pltpu

HEAD_DIM = 64
N_DIFF_HEADS = 4
N_SB_HEADS = 8
DIFF_WIDTH = N_DIFF_HEADS * 2 * HEAD_DIM
SB_WIDTH = N_SB_HEADS * HEAD_DIM
CHUNK = 64
ROPE_THETA = 10000.0
NORM_EPS = 1e-6
N_MOD = 6

LANES = 128
SUBLANES = 8
ATT_BLOCK = 256
SEG_LEN = ATT_BLOCK // SUBLANES
TOKEN_TILE = 512
VMEM_LIMIT = 56 * 1024 * 1024

_NT = (((1,), (1,)), ((), ()))
_NN = (((1,), (0,)), ((), ()))


def _rms_scale(xf):
    return lax.rsqrt(jnp.mean(xf * xf, axis=-1, keepdims=True) + NORM_EPS)


def _adaln_kernel(c_ref, w_ref, b_ref, o_ref):
    c = c_ref[...]
    ca = c / (1.0 + jnp.exp(-c))
    o_ref[...] = lax.dot_general(ca.astype(jnp.bfloat16), w_ref[...].astype(jnp.bfloat16), _NN,
                                 preferred_element_type=jnp.float32) + b_ref[...]


def _adaln_call(c, w, b):
    bsz, d = c.shape
    n = w.shape[1]
    tn = 1024
    return pl.pallas_call(
        _adaln_kernel,
        grid=(n // tn,),
        in_specs=[pl.BlockSpec((bsz, d), lambda j: (0, 0)),
                  pl.BlockSpec((d, tn), lambda j: (0, j)),
                  pl.BlockSpec((1, tn), lambda j: (0, j))],
        out_specs=pl.BlockSpec((bsz, tn), lambda j: (0, j)),
        out_shape=jax.ShapeDtypeStruct((bsz, n), jnp.float32),
        compiler_params=pltpu.CompilerParams(dimension_semantics=("arbitrary",),
                                             vmem_limit_bytes=VMEM_LIMIT),
        name="adaln",
    )(c, w, b.reshape(1, n))


def _inproj_kernel(x_ref, mod_ref, g_ref, w_ref, cos_ref, sin_ref, o_ref):
    xf = x_ref[0]
    sh = mod_ref[0, 0:1, :]
    sc = mod_ref[0, 1:2, :]
    h = (xf * _rms_scale(xf) * g_ref[...]) * (1.0 + sc) + sh
    hb = h.astype(jnp.bfloat16)
    cos = cos_ref[...]
    sin = sin_ref[...]
    lane = lax.broadcasted_iota(jnp.int32, cos.shape, 1)
    first_half = (lane % HEAD_DIM) < (HEAD_DIM // 2)
    scale = HEAD_DIM ** -0.5
    gw = DIFF_WIDTH
    for n in range(6):
        acc = lax.dot_general(hb, w_ref[:, n * gw:(n + 1) * gw], _NN,
                              preferred_element_type=jnp.float32)
        if n < 2:
            for j in range(gw // LANES):
                xb = acc[:, j * LANES:(j + 1) * LANES]
                rot = jnp.where(first_half, pltpu.roll(xb, LANES - HEAD_DIM // 2, 1),
                                pltpu.roll(xb, HEAD_DIM // 2, 1))
                yb = xb * cos + rot * sin
                if n == 0:
                    yb = yb * scale
                o_ref[0, :, n * gw + j * LANES:n * gw + (j + 1) * LANES] = yb.astype(o_ref.dtype)
        else:
            if n == 3:
                acc = acc * (-scale)
            o_ref[0, :, n * gw:(n + 1) * gw] = acc.astype(o_ref.dtype)


def _inproj_call(x, mod3, g, w_bf16, cos128, sin128):
    bsz, seq, d = x.shape
    n = w_bf16.shape[1]
    tm = TOKEN_TILE
    const = dict(pipeline_mode=pl.Buffered(1))
    return pl.pallas_call(
        _inproj_kernel,
        grid=(bsz, seq // tm),
        in_specs=[pl.BlockSpec((1, tm, d), lambda b, t: (b, t, 0)),
                  pl.BlockSpec((1, N_MOD, d), lambda b, t: (b, 0, 0)),
                  pl.BlockSpec((1, d), lambda b, t: (0, 0)),
                  pl.BlockSpec((d, n), lambda b, t: (0, 0), **const),
                  pl.BlockSpec((tm, LANES), lambda b, t: (t, 0)),
                  pl.BlockSpec((tm, LANES), lambda b, t: (t, 0))],
        out_specs=pl.BlockSpec((1, tm, n), lambda b, t: (b, t, 0)),
        out_shape=jax.ShapeDtypeStruct((bsz, seq, n), jnp.bfloat16),
        compiler_params=pltpu.CompilerParams(dimension_semantics=("arbitrary", "arbitrary"),
                                             vmem_limit_bytes=VMEM_LIMIT),
        name="inproj",
    )(x, mod3, g.reshape(1, d), w_bf16, cos128, sin128)


def _diff_kernel(lq1_ref, lk1_ref, lq2_ref, lk2_ref, g_ref, q_ref, k_ref, v_ref, o_ref,
                 vt_scr, acc_scr, m_scr, l_scr, *, lambda_init):
    i = pl.program_id(2)
    nblk = v_ref.shape[1]
    tb = ATT_BLOCK

    @pl.when(i == 0)
    def _():
        for blk in range(nblk):
            vt_scr[blk] = v_ref[0, blk].astype(jnp.float32).T.astype(jnp.bfloat16)

    q = q_ref[0, 0]
    lane = lax.broadcasted_iota(jnp.int32, q.shape, 1)
    zero = jnp.zeros_like(q)
    q_maps = (jnp.where(lane < HEAD_DIM, q, zero), jnp.where(lane >= HEAD_DIM, q, zero))

    row = lax.broadcasted_iota(jnp.int32, (tb, tb), 0)
    col = lax.broadcasted_iota(jnp.int32, (tb, tb), 1)
    chunk_mask = (row // CHUNK) <= (col // CHUNK)

    def tile(j, first):
        kt = k_ref[0, j]
        vt = vt_scr[j]
        for mp in range(2):
            s_t = lax.dot_general(kt, q_maps[mp], _NT, preferred_element_type=jnp.float32)
            if first:
                s_t = jnp.where(chunk_mask, s_t, -jnp.inf)
                m_new = jnp.max(s_t, axis=0, keepdims=True)
                p = jnp.exp(s_t - m_new)
                l_scr[mp] = jnp.sum(p, axis=0, keepdims=True)
                acc_scr[mp] = lax.dot_general(vt, p.astype(jnp.bfloat16), _NN,
                                              preferred_element_type=jnp.float32)
            else:
                m_old = m_scr[mp]
                m_new = jnp.maximum(m_old, jnp.max(s_t, axis=0, keepdims=True))
                alpha = jnp.exp(m_old - m_new)
                p = jnp.exp(s_t - m_new)
                l_scr[mp] = alpha * l_scr[mp] + jnp.sum(p, axis=0, keepdims=True)
                acc_scr[mp] = alpha * acc_scr[mp] + lax.dot_general(
                    vt, p.astype(jnp.bfloat16), _NN, preferred_element_type=jnp.float32)
            m_scr[mp] = m_new

    tile(i, True)

    def body(t, carry):
        tile(t, False)
        return carry

    lax.fori_loop(0, i, body, 0)

    lam = (jnp.exp(jnp.sum(lq1_ref[...] * lk1_ref[...], axis=-1, keepdims=True))
           - jnp.exp(jnp.sum(lq2_ref[...] * lk2_ref[...], axis=-1, keepdims=True)) + lambda_init)
    o_t = acc_scr[0] / l_scr[0] - lam * (acc_scr[1] / l_scr[1])
    ms = jnp.mean(o_t * o_t, axis=0, keepdims=True)
    o_t = o_t * lax.rsqrt(ms + NORM_EPS) * g_ref[...] * (1.0 - lambda_init)
    o_ref[0] = o_t.T.astype(o_ref.dtype)


def _diff_attn_call(proj4, lq1, lk1, lq2, lk2, subln_g, lambda_init):
    bsz, nblk, tb, _ = proj4.shape
    seq = nblk * tb
    vec = lambda a: a.reshape(1, HEAD_DIM)
    small = pl.BlockSpec((1, HEAD_DIM), lambda b, h, i: (0, 0))
    return pl.pallas_call(
        functools.partial(_diff_kernel, lambda_init=lambda_init),
        grid=(bsz, N_DIFF_HEADS, nblk),
        in_specs=[small, small, small, small,
                  pl.BlockSpec((2 * HEAD_DIM, 1), lambda b, h, i: (0, 0)),
                  pl.BlockSpec((1, 1, tb, LANES), lambda b, h, i: (b, i, 0, h)),
                  pl.BlockSpec((1, nblk, tb, LANES), lambda b, h, i: (b, 0, 0, N_DIFF_HEADS + h)),
                  pl.BlockSpec((1, nblk, tb, LANES), lambda b, h, i: (b, 0, 0, 2 * N_DIFF_HEADS + h))],
        out_specs=pl.BlockSpec((1, tb, LANES), lambda b, h, i: (b, i, h)),
        out_shape=jax.ShapeDtypeStruct((bsz, seq, DIFF_WIDTH), jnp.bfloat16),
        scratch_shapes=[pltpu.VMEM((nblk, LANES, tb), jnp.bfloat16),
                        pltpu.VMEM((2, LANES, tb), jnp.float32),
                        pltpu.VMEM((2, 1, tb), jnp.float32),
                        pltpu.VMEM((2, 1, tb), jnp.float32)],
        compiler_params=pltpu.CompilerParams(
            dimension_semantics=("arbitrary", "arbitrary", "arbitrary"), vmem_limit_bytes=VMEM_LIMIT),
        name="diff_attn",
    )(vec(lq1), vec(lk1), vec(lq2), vec(lk2), subln_g.reshape(2 * HEAD_DIM, 1), proj4, proj4, proj4)


def _sb_kernel(q_ref, k_ref, v_ref, o_ref, kp_scr, vt_scr, acc_scr):
    i = pl.program_id(2)
    nblk = k_ref.shape[1]
    tb = ATT_BLOCK

    row = lax.broadcasted_iota(jnp.int32, (tb, tb), 0)
    col = lax.broadcasted_iota(jnp.int32, (tb, tb), 1)
    pos = SEG_LEN * (row % SUBLANES) + row // SUBLANES

    @pl.when(i == 0)
    def _():
        perm = jnp.where(col == pos, 1.0, 0.0).astype(jnp.bfloat16)
        for blk in range(nblk):
            kp = lax.dot_general(perm, k_ref[0, blk], _NN, preferred_element_type=jnp.float32)
            kp_scr[blk] = kp.astype(jnp.bfloat16)
            vp = lax.dot_general(perm, v_ref[0, blk], _NN, preferred_element_type=jnp.float32)
            vt_scr[blk] = vp.T.astype(jnp.bfloat16)

    q = q_ref[0, 0]
    lane = lax.broadcasted_iota(jnp.int32, q.shape, 1)
    zero = jnp.zeros_like(q)
    q_heads = (jnp.where(lane < HEAD_DIM, q, zero), jnp.where(lane >= HEAD_DIM, q, zero))
    causal = pos < col
    sub = lax.broadcasted_iota(jnp.int32, (SUBLANES, tb), 0)

    def head_tile(kt, vt, qh, carry, diag):
        nz = lax.dot_general(kt, qh, _NT, preferred_element_type=jnp.float32)
        beta = 1.0 / (1.0 + jnp.exp(nz))
        if diag:
            beta = jnp.where(causal, beta, 0.0)
        rem = jnp.ones((SUBLANES, tb), jnp.float32)
        parts = [None] * SEG_LEN
        for v in range(SEG_LEN - 1, -1, -1):
            a_v = beta[v * SUBLANES:(v + 1) * SUBLANES, :] * rem
            rem = rem - a_v
            parts[v] = a_v
        incl = rem
        for k in (1, 2, 4):
            shifted = pltpu.roll(incl, SUBLANES - k, 0)
            incl = incl * jnp.where(sub + k < SUBLANES, shifted, 1.0)
        excl = jnp.where(sub + 1 < SUBLANES, pltpu.roll(incl, SUBLANES - 1, 0), 1.0)
        start = carry * excl
        a = jnp.concatenate([p * start for p in parts], axis=0).astype(jnp.bfloat16)
        out = lax.dot_general(vt, a, _NN, preferred_element_type=jnp.float32)
        new_carry = carry * jnp.broadcast_to(incl[0:1, :], carry.shape)
        return out, new_carry

    def tile(j, carries, diag):
        kt = kp_scr[j]
        vt = vt_scr[j]
        new = []
        for hh in range(2):
            out, c = head_tile(kt, vt[hh * HEAD_DIM:(hh + 1) * HEAD_DIM, :], q_heads[hh], carries[hh], diag)
            rows = slice(hh * HEAD_DIM, (hh + 1) * HEAD_DIM)
            if diag:
                acc_scr[rows, :] = out
            else:
                acc_scr[rows, :] += out
            new.append(c)
        return tuple(new)

    ones = jnp.ones((SUBLANES, tb), jnp.float32)
    carries = tile(i, (ones, ones), True)

    def body(t, carries):
        return tile(i - 1 - t, carries, False)

    lax.fori_loop(0, i, body, carries)
    o_ref[0] = acc_scr[...].T.astype(o_ref.dtype)


def _sb_attn_call(proj4):
    bsz, nblk, tb, _ = proj4.shape
    seq = nblk * tb
    base = 3 * DIFF_WIDTH // LANES
    pairs = SB_WIDTH // LANES
    return pl.pallas_call(
        _sb_kernel,
        grid=(bsz, pairs, nblk),
        in_specs=[pl.BlockSpec((1, 1, tb, LANES), lambda b, h, i: (b, i, 0, base + h)),
                  pl.BlockSpec((1, nblk, tb, LANES), lambda b, h, i: (b, 0, 0, base + pairs + h)),
                  pl.BlockSpec((1, nblk, tb, LANES), lambda b, h, i: (b, 0, 0, base + 2 * pairs + h))],
        out_specs=pl.BlockSpec((1, tb, LANES), lambda b, h, i: (b, i, h)),
        out_shape=jax.ShapeDtypeStruct((bsz, seq, SB_WIDTH), jnp.bfloat16),
        scratch_shapes=[pltpu.VMEM((nblk, tb, LANES), jnp.bfloat16),
                        pltpu.VMEM((nblk, LANES, tb), jnp.bfloat16),
                        pltpu.VMEM((LANES, tb), jnp.float32)],
        compiler_params=pltpu.CompilerParams(
            dimension_semantics=("arbitrary", "arbitrary", "arbitrary"), vmem_limit_bytes=VMEM_LIMIT),
        name="sb_attn",
    )(proj4, proj4, proj4)


def _out_ffn_kernel(x_ref, od_ref, os_ref, mod_ref, wo_ref, gf_ref, w1_ref, w2_ref, gl_ref, o_ref):
    g_m = mod_ref[0, 2:3, :]
    sh_f = mod_ref[0, 3:4, :]
    sc_f = mod_ref[0, 4:5, :]
    g_f = mod_ref[0, 5:6, :]
    mixed = (lax.dot_general(od_ref[0], wo_ref[:DIFF_WIDTH, :], _NN, preferred_element_type=jnp.float32)
             + lax.dot_general(os_ref[0], wo_ref[DIFF_WIDTH:, :], _NN, preferred_element_type=jnp.float32))
    x1 = x_ref[0] + g_m * mixed
    h = ((x1 * _rms_scale(x1) * gf_ref[...]) * (1.0 + sc_f) + sh_f).astype(jnp.bfloat16)
    d_ff = w1_ref.shape[1]
    fc = 1024
    f = None
    for c in range(d_ff // fc):
        u = lax.dot_general(h, w1_ref[:, c * fc:(c + 1) * fc], _NN, preferred_element_type=jnp.float32)
        r = jnp.square(jnp.maximum(u, 0.0)).astype(jnp.bfloat16)
        part = lax.dot_general(r, w2_ref[c * fc:(c + 1) * fc, :], _NN, preferred_element_type=jnp.float32)
        f = part if f is None else f + part
    x2 = x1 + g_f * f
    o_ref[0] = x2 * _rms_scale(x2) * gl_ref[...]


def _out_ffn_call(x, o_diff, o_sb, mod3, wo, gf, w1, w2, gl):
    bsz, seq, d = x.shape
    tm = TOKEN_TILE
    const = dict(pipeline_mode=pl.Buffered(1))
    tok = lambda w: pl.BlockSpec((1, tm, w), lambda b, t: (b, t, 0))
    return pl.pallas_call(
        _out_ffn_kernel,
        grid=(bsz, seq // tm),
        in_specs=[tok(d), tok(DIFF_WIDTH), tok(SB_WIDTH),
                  pl.BlockSpec((1, N_MOD, d), lambda b, t: (b, 0, 0)),
                  pl.BlockSpec(wo.shape, lambda b, t: (0, 0), **const),
                  pl.BlockSpec((1, d), lambda b, t: (0, 0)),
                  pl.BlockSpec(w1.shape, lambda b, t: (0, 0), **const),
                  pl.BlockSpec(w2.shape, lambda b, t: (0, 0), **const),
                  pl.BlockSpec((1, d), lambda b, t: (0, 0))],
        out_specs=tok(d),
        out_shape=jax.ShapeDtypeStruct((bsz, seq, d), jnp.float32),
        compiler_params=pltpu.CompilerParams(dimension_semantics=("arbitrary", "arbitrary"),
                                             vmem_limit_bytes=VMEM_LIMIT),
        name="out_ffn",
    )(x, o_diff, o_sb, mod3, wo, gf.reshape(1, d), w1, w2, gl.reshape(1, d))


def _rope_tables(seq_len):
    dim = HEAD_DIM
    inv = 1.0 / (ROPE_THETA ** (jnp.arange(0, dim, 2, dtype=jnp.float32) / dim))
    ang = jnp.arange(seq_len, dtype=jnp.float32)[:, None] * inv[None, :]
    ang = jnp.concatenate([ang, ang], axis=-1)
    cos, sin = jnp.cos(ang), jnp.sin(ang)
    sign = jnp.where(jnp.arange(dim) < dim // 2, -1.0, 1.0).astype(jnp.float32)
    reps = LANES // dim
    return jnp.tile(cos, (1, reps)), jnp.tile(sin * sign[None, :], (1, reps))


def kernel(x, c, ada_w, ada_b, mix_norm_g, w_in, lambda_q1, lambda_k1, lambda_q2, lambda_k2,
           diff_subln_g, w_out, ffn_norm_g, w_ff1, w_ff2, final_norm_g):
    bsz, seq, d = x.shape
    depth = ada_w.shape[0]
    assert depth == 1, "the fused final norm assumes a single layer"
    assert seq % TOKEN_TILE == 0 and seq % ATT_BLOCK == 0
    cos128, sin128 = _rope_tables(seq)
    bf = jnp.bfloat16
    layer = 0
    lambda_init = 0.8 - 0.6 * math.exp(-0.3 * layer)
    mod3 = _adaln_call(c, ada_w[layer], ada_b[layer]).reshape(bsz, N_MOD, d)
    proj = _inproj_call(x, mod3, mix_norm_g[layer], w_in[layer].astype(bf), cos128, sin128)
    proj4 = proj.reshape(bsz, seq // ATT_BLOCK, ATT_BLOCK, proj.shape[-1])
    o_diff = _diff_attn_call(proj4, lambda_q1[layer], lambda_k1[layer], lambda_q2[layer],
                             lambda_k2[layer], diff_subln_g[layer], lambda_init)
    o_sb = _sb_attn_call(proj4)
    return _out_ffn_call(x, o_diff, o_sb, mod3, w_out[layer].astype(bf), ffn_norm_g[layer],
                         w_ff1[layer].astype(bf), w_ff2[layer].astype(bf), final_norm_g)
```

```python
import functools
import math

import jax
import jax.numpy as jnp
from jax import lax
from jax.experimental import pallas as pl
from jax.experimental.pallas import tpu as pltpu

HEAD_DIM = 64
N_DIFF_HEADS = 4
N_SB_HEADS = 8
DIFF_WIDTH = N_DIFF_HEADS * 2 * HEAD_DIM
SB_WIDTH = N_SB_HEADS * HEAD_DIM
CHUNK = 64
ROPE_THETA = 10000.0
NORM_EPS = 1e-6
N_MOD = 6

LANES = 128
SUBLANES = 8
ATT_BLOCK = 256
SEG_LEN = ATT_BLOCK // SUBLANES
TOKEN_TILE = 512
N_ITEMS = 8
ATT_PREFETCH = 3
ATT_ISSUE = ((3,), (4, 5), (6, 7), (8,), (9,), (10,), (), ())
VMEM_LIMIT = 56 * 1024 * 1024

_NT = (((1,), (1,)), ((), ()))
_NN = (((1,), (0,)), ((), ()))


def _rms_scale(xf):
    return lax.rsqrt(jnp.mean(xf * xf, axis=-1, keepdims=True) + NORM_EPS)


def _adaln_kernel(c_ref, w_ref, b_ref, o_ref):
    c = c_ref[...]
    ca = c / (1.0 + jnp.exp(-c))
    o_ref[...] = lax.dot_general(ca.astype(jnp.bfloat16), w_ref[...].astype(jnp.bfloat16), _NN,
                                 preferred_element_type=jnp.float32) + b_ref[...]


def _adaln_call(c, w, b):
    bsz, d = c.shape
    n = w.shape[1]
    tn = 1024
    return pl.pallas_call(
        _adaln_kernel,
        grid=(n // tn,),
        in_specs=[pl.BlockSpec((bsz, d), lambda j: (0, 0)),
                  pl.BlockSpec((d, tn), lambda j: (0, j)),
                  pl.BlockSpec((1, tn), lambda j: (0, j))],
        out_specs=pl.BlockSpec((bsz, tn), lambda j: (0, j)),
        out_shape=jax.ShapeDtypeStruct((bsz, n), jnp.float32),
        compiler_params=pltpu.CompilerParams(dimension_semantics=("arbitrary",),
                                             vmem_limit_bytes=VMEM_LIMIT),
        name="adaln",
    )(c, w, b.reshape(1, n))


def _inproj_kernel(x_ref, mod_ref, g_ref, w_ref, cos_ref, sin_ref, o_ref):
    xf = x_ref[0]
    sh = mod_ref[0, 0:1, :]
    sc = mod_ref[0, 1:2, :]
    h = (xf * _rms_scale(xf) * g_ref[...]) * (1.0 + sc) + sh
    hb = h.astype(jnp.bfloat16)
    cos = cos_ref[...]
    sin = sin_ref[...]
    lane = lax.broadcasted_iota(jnp.int32, cos.shape, 1)
    first_half = (lane % HEAD_DIM) < (HEAD_DIM // 2)
    scale = HEAD_DIM ** -0.5
    gw = DIFF_WIDTH
    for n in range(6):
        acc = lax.dot_general(hb, w_ref[:, n * gw:(n + 1) * gw], _NN,
                              preferred_element_type=jnp.float32)
        if n < 2:
            for j in range(gw // LANES):
                xb = acc[:, j * LANES:(j + 1) * LANES]
                rot = jnp.where(first_half, pltpu.roll(xb, LANES - HEAD_DIM // 2, 1),
                                pltpu.roll(xb, HEAD_DIM // 2, 1))
                yb = xb * cos + rot * sin
                if n == 0:
                    yb = yb * scale
                o_ref[0, :, n * gw + j * LANES:n * gw + (j + 1) * LANES] = yb.astype(o_ref.dtype)
        else:
            if n == 3:
                acc = acc * (0.5 * scale)
            o_ref[0, :, n * gw:(n + 1) * gw] = acc.astype(o_ref.dtype)


def _inproj_call(x, mod3, g, w_bf16, cos128, sin128):
    bsz, seq, d = x.shape
    n = w_bf16.shape[1]
    tm = TOKEN_TILE
    const = dict(pipeline_mode=pl.Buffered(1))
    return pl.pallas_call(
        _inproj_kernel,
        grid=(bsz, seq // tm),
        in_specs=[pl.BlockSpec((1, tm, d), lambda b, t: (b, t, 0)),
                  pl.BlockSpec((1, N_MOD, d), lambda b, t: (b, 0, 0)),
                  pl.BlockSpec((1, d), lambda b, t: (0, 0)),
                  pl.BlockSpec((d, n), lambda b, t: (0, 0), **const),
                  pl.BlockSpec((tm, LANES), lambda b, t: (t, 0)),
                  pl.BlockSpec((tm, LANES), lambda b, t: (t, 0))],
        out_specs=pl.BlockSpec((1, tm, n), lambda b, t: (b, t, 0)),
        out_shape=jax.ShapeDtypeStruct((bsz, seq, n), jnp.bfloat16),
        compiler_params=pltpu.CompilerParams(dimension_semantics=("arbitrary", "arbitrary"),
                                             vmem_limit_bytes=VMEM_LIMIT),
        name="inproj",
    )(x, mod3, g.reshape(1, d), w_bf16, cos128, sin128)


def _diff_kernel(lq1_ref, lk1_ref, lq2_ref, lk2_ref, g_ref, q_ref, k_ref, v_ref, o_ref,
                 vt_scr, qm_scr, acc_scr, m_scr, l_scr, pre_scr, s_scr, *, lambda_init):
    i = pl.program_id(1)
    nblk = v_ref.shape[1]
    tb = ATT_BLOCK
    vregs = tb // SUBLANES

    @pl.when(i == 0)
    def _():
        for blk in range(nblk):
            vt_scr[blk] = v_ref[0, blk].astype(jnp.float32).T.astype(jnp.bfloat16)

    lane = lax.broadcasted_iota(jnp.int32, (tb, LANES), 1)
    for it in range(N_ITEMS):
        qp = q_ref[0, 0, :, (it // 2) * LANES:(it // 2 + 1) * LANES]
        keep = (lane < HEAD_DIM) if it % 2 == 0 else (lane >= HEAD_DIM)
        qm_scr[it] = jnp.where(keep, qp, jnp.zeros_like(qp))

    qcol = lax.broadcasted_iota(jnp.int32, (SUBLANES, tb), 1)

    def scores(j, it):
        hd = it // 2
        return lax.dot_general(k_ref[0, j, :, hd * LANES:(hd + 1) * LANES], qm_scr[it], _NT,
                               preferred_element_type=jnp.float32)

    def elementwise(s_t, it, first):
        buf = it % 2
        mx = None
        for v in range(vregs):
            sv = s_t[v * SUBLANES:(v + 1) * SUBLANES, :]
            if first:
                sv = jnp.where(qcol >= (v * SUBLANES // CHUNK) * CHUNK, sv, -jnp.inf)
            mx = sv if mx is None else jnp.maximum(mx, sv)
            s_scr[buf, v * SUBLANES:(v + 1) * SUBLANES, :] = sv
        tmax = jnp.max(mx, axis=0, keepdims=True)
        if first:
            m_new = tmax
        else:
            m_old = m_scr[it]
            m_new = jnp.maximum(m_old, tmax)
            alpha = jnp.exp(m_old - m_new)
        m_scr[it] = m_new
        ls = None
        parts = []
        for v in range(vregs):
            p = jnp.exp(s_scr[buf, v * SUBLANES:(v + 1) * SUBLANES, :] - m_new)
            ls = p if ls is None else ls + p
            parts.append(p)
        lsum = jnp.sum(ls, axis=0, keepdims=True)
        if first:
            l_scr[it] = lsum
            return jnp.concatenate(parts, axis=0).astype(jnp.bfloat16), None
        l_scr[it] = alpha * l_scr[it] + lsum
        return jnp.concatenate(parts, axis=0).astype(jnp.bfloat16), alpha

    def tile(j, j_next, first):
        ss = [None] * N_ITEMS
        for it in range(ATT_PREFETCH):
            ss[it] = pre_scr[it]
        for it in range(N_ITEMS):
            for idx in ATT_ISSUE[it]:
                if idx < N_ITEMS:
                    ss[idx] = scores(j, idx)
                else:
                    pre_scr[idx - N_ITEMS] = scores(j_next, idx - N_ITEMS)
            p, alpha = elementwise(ss[it], it, first)
            hd = it // 2
            pv = lax.dot_general(vt_scr[j, hd * LANES:(hd + 1) * LANES, :], p, _NN,
                                 preferred_element_type=jnp.float32)
            if first:
                acc_scr[it] = pv
            else:
                acc_scr[it] = alpha * acc_scr[it] + pv

    for it in range(ATT_PREFETCH):
        pre_scr[it] = scores(i, it)
    tile(i, jnp.maximum(i - 1, 0), True)

    def body(t, c):
        j = i - 1 - t
        tile(j, jnp.maximum(j - 1, 0), False)
        return c

    lax.fori_loop(0, i, body, 0)

    lam = (jnp.exp(jnp.sum(lq1_ref[...] * lk1_ref[...], axis=-1, keepdims=True))
           - jnp.exp(jnp.sum(lq2_ref[...] * lk2_ref[...], axis=-1, keepdims=True)) + lambda_init)
    for hd in range(N_DIFF_HEADS):
        o_t = (acc_scr[2 * hd] / l_scr[2 * hd]
               - lam * (acc_scr[2 * hd + 1] / l_scr[2 * hd + 1]))
        ms = jnp.mean(o_t * o_t, axis=0, keepdims=True)
        o_t = o_t * lax.rsqrt(ms + NORM_EPS) * g_ref[...] * (1.0 - lambda_init)
        o_ref[0, :, hd * LANES:(hd + 1) * LANES] = o_t.T.astype(o_ref.dtype)


def _diff_attn_call(proj4, lq1, lk1, lq2, lk2, subln_g, lambda_init):
    bsz, nblk, tb, _ = proj4.shape
    seq = nblk * tb
    w = DIFF_WIDTH
    vec = lambda a: a.reshape(1, HEAD_DIM)
    small = pl.BlockSpec((1, HEAD_DIM), lambda b, i: (0, 0))
    return pl.pallas_call(
        functools.partial(_diff_kernel, lambda_init=lambda_init),
        grid=(bsz, nblk),
        in_specs=[small, small, small, small,
                  pl.BlockSpec((2 * HEAD_DIM, 1), lambda b, i: (0, 0)),
                  pl.BlockSpec((1, 1, tb, w), lambda b, i: (b, i, 0, 0)),
                  pl.BlockSpec((1, nblk, tb, w), lambda b, i: (b, 0, 0, 1)),
                  pl.BlockSpec((1, nblk, tb, w), lambda b, i: (b, 0, 0, 2))],
        out_specs=pl.BlockSpec((1, tb, w), lambda b, i: (b, i, 0)),
        out_shape=jax.ShapeDtypeStruct((bsz, seq, w), jnp.bfloat16),
        scratch_shapes=[pltpu.VMEM((nblk, w, tb), jnp.bfloat16),
                        pltpu.VMEM((N_ITEMS, tb, LANES), jnp.bfloat16),
                        pltpu.VMEM((N_ITEMS, LANES, tb), jnp.float32),
                        pltpu.VMEM((N_ITEMS, 1, tb), jnp.float32),
                        pltpu.VMEM((N_ITEMS, 1, tb), jnp.float32),
                        pltpu.VMEM((ATT_PREFETCH, tb, tb), jnp.float32),
                        pltpu.VMEM((2, tb, tb), jnp.float32)],
        compiler_params=pltpu.CompilerParams(
            dimension_semantics=("arbitrary", "arbitrary"), vmem_limit_bytes=VMEM_LIMIT),
        name="diff_attn",
    )(vec(lq1), vec(lk1), vec(lq2), vec(lk2), subln_g.reshape(2 * HEAD_DIM, 1), proj4, proj4, proj4)


def _sb_kernel(q_ref, k_ref, v_ref, o_ref, kp_scr, vt_scr, qm_scr, acc_scr, pre_scr, ap_scr, carry_scr):
    i = pl.program_id(1)
    nblk = k_ref.shape[1]
    tb = ATT_BLOCK

    @pl.when(i == 0)
    def _():
        row = lax.broadcasted_iota(jnp.int32, (tb, tb), 0)
        col = lax.broadcasted_iota(jnp.int32, (tb, tb), 1)
        pos = SEG_LEN * (row % SUBLANES) + (SEG_LEN - 1 - row // SUBLANES)
        perm = jnp.where(col == pos, 1.0, 0.0).astype(jnp.bfloat16)
        for blk in range(nblk):
            kp = lax.dot_general(perm, k_ref[0, blk], _NN, preferred_element_type=jnp.float32)
            kp_scr[blk] = kp.astype(jnp.bfloat16)
            vp = lax.dot_general(perm, v_ref[0, blk], _NN, preferred_element_type=jnp.float32)
            vt_scr[blk] = vp.T.astype(jnp.bfloat16)

    lane = lax.broadcasted_iota(jnp.int32, (tb, LANES), 1)
    for h in range(N_ITEMS):
        qp = q_ref[0, 0, :, (h // 2) * LANES:(h // 2 + 1) * LANES]
        keep = (lane < HEAD_DIM) if h % 2 == 0 else (lane >= HEAD_DIM)
        qm_scr[h] = jnp.where(keep, qp, jnp.zeros_like(qp))
        carry_scr[h] = jnp.ones((SUBLANES, tb), jnp.float32)

    sub = lax.broadcasted_iota(jnp.int32, (SUBLANES, tb), 0)
    qcol = lax.broadcasted_iota(jnp.int32, (SUBLANES, tb), 1)

    def scores(j, h):
        pair = h // 2
        return lax.dot_general(kp_scr[j, :, pair * LANES:(pair + 1) * LANES], qm_scr[h], _NT,
                               preferred_element_type=jnp.float32)

    def elementwise(zh, h, diag):
        buf = h % 2
        rem = jnp.ones((SUBLANES, tb), jnp.float32)
        for v in range(SEG_LEN):
            beta = 0.5 * jnp.tanh(zh[v * SUBLANES:(v + 1) * SUBLANES, :]) + 0.5
            if diag:
                beta = jnp.where(SEG_LEN * sub + (SEG_LEN - 1 - v) < qcol, beta, 0.0)
            a_v = beta * rem
            rem = rem - a_v
            ap_scr[buf, v * SUBLANES:(v + 1) * SUBLANES, :] = a_v
        incl = rem
        for k in (1, 2, 4):
            shifted = pltpu.roll(incl, SUBLANES - k, 0)
            incl = incl * jnp.where(sub + k < SUBLANES, shifted, 1.0)
        excl = jnp.where(sub + 1 < SUBLANES, pltpu.roll(incl, SUBLANES - 1, 0), 1.0)
        carry = carry_scr[h]
        start = carry * excl
        carry_scr[h] = carry * jnp.broadcast_to(incl[0:1, :], carry.shape)
        return jnp.concatenate(
            [ap_scr[buf, v * SUBLANES:(v + 1) * SUBLANES, :] * start for v in range(SEG_LEN)],
            axis=0).astype(jnp.bfloat16)

    def tile(j, j_next, diag):
        zs = [None] * N_ITEMS
        for h in range(ATT_PREFETCH):
            zs[h] = pre_scr[h]
        for h in range(N_ITEMS):
            for idx in ATT_ISSUE[h]:
                if idx < N_ITEMS:
                    zs[idx] = scores(j, idx)
                else:
                    pre_scr[idx - N_ITEMS] = scores(j_next, idx - N_ITEMS)
            a = elementwise(zs[h], h, diag)
            rows = slice(h * HEAD_DIM, (h + 1) * HEAD_DIM)
            out = lax.dot_general(vt_scr[j, rows, :], a, _NN, preferred_element_type=jnp.float32)
            if diag:
                acc_scr[rows, :] = out
            else:
                acc_scr[rows, :] += out

    for h in range(ATT_PREFETCH):
        pre_scr[h] = scores(i, h)
    tile(i, jnp.maximum(i - 1, 0), True)

    def body(t, c):
        j = i - 1 - t
        tile(j, jnp.maximum(j - 1, 0), False)
        return c

    lax.fori_loop(0, i, body, 0)
    o_ref[0] = acc_scr[...].T.astype(o_ref.dtype)


def _sb_attn_call(proj4):
    bsz, nblk, tb, _ = proj4.shape
    seq = nblk * tb
    w = SB_WIDTH
    base = 3 * DIFF_WIDTH // w
    return pl.pallas_call(
        _sb_kernel,
        grid=(bsz, nblk),
        in_specs=[pl.BlockSpec((1, 1, tb, w), lambda b, i: (b, i, 0, base)),
                  pl.BlockSpec((1, nblk, tb, w), lambda b, i: (b, 0, 0, base + 1)),
                  pl.BlockSpec((1, nblk, tb, w), lambda b, i: (b, 0, 0, base + 2))],
        out_specs=pl.BlockSpec((1, tb, w), lambda b, i: (b, i, 0)),
        out_shape=jax.ShapeDtypeStruct((bsz, seq, w), jnp.bfloat16),
        scratch_shapes=[pltpu.VMEM((nblk, tb, w), jnp.bfloat16),
                        pltpu.VMEM((nblk, w, tb), jnp.bfloat16),
                        pltpu.VMEM((N_ITEMS, tb, LANES), jnp.bfloat16),
                        pltpu.VMEM((w, tb), jnp.float32),
                        pltpu.VMEM((ATT_PREFETCH, tb, tb), jnp.float32),
                        pltpu.VMEM((2, tb, tb), jnp.float32),
                        pltpu.VMEM((N_ITEMS, SUBLANES, tb), jnp.float32)],
        compiler_params=pltpu.CompilerParams(
            dimension_semantics=("arbitrary", "arbitrary"), vmem_limit_bytes=VMEM_LIMIT),
        name="sb_attn",
    )(proj4, proj4, proj4)


def _out_ffn_kernel(x_ref, od_ref, os_ref, mod_ref, wo_ref, gf_ref, w1_ref, w2_ref, gl_ref, o_ref):
    g_m = mod_ref[0, 2:3, :]
    sh_f = mod_ref[0, 3:4, :]
    sc_f = mod_ref[0, 4:5, :]
    g_f = mod_ref[0, 5:6, :]
    mixed = (lax.dot_general(od_ref[0], wo_ref[:DIFF_WIDTH, :], _NN, preferred_element_type=jnp.float32)
             + lax.dot_general(os_ref[0], wo_ref[DIFF_WIDTH:, :], _NN, preferred_element_type=jnp.float32))
    x1 = x_ref[0] + g_m * mixed
    h = ((x1 * _rms_scale(x1) * gf_ref[...]) * (1.0 + sc_f) + sh_f).astype(jnp.bfloat16)
    d_ff = w1_ref.shape[1]
    fc = 1024
    f = None
    for c in range(d_ff // fc):
        u = lax.dot_general(h, w1_ref[:, c * fc:(c + 1) * fc], _NN, preferred_element_type=jnp.float32)
        r = jnp.square(jnp.maximum(u, 0.0)).astype(jnp.bfloat16)
        part = lax.dot_general(r, w2_ref[c * fc:(c + 1) * fc, :], _NN, preferred_element_type=jnp.float32)
        f = part if f is None else f + part
    x2 = x1 + g_f * f
    o_ref[0] = x2 * _rms_scale(x2) * gl_ref[...]


def _out_ffn_call(x, o_diff, o_sb, mod3, wo, gf, w1, w2, gl):
    bsz, seq, d = x.shape
    tm = TOKEN_TILE
    const = dict(pipeline_mode=pl.Buffered(1))
    tok = lambda w: pl.BlockSpec((1, tm, w), lambda b, t: (b, t, 0))
    return pl.pallas_call(
        _out_ffn_kernel,
        grid=(bsz, seq // tm),
        in_specs=[tok(d), tok(DIFF_WIDTH), tok(SB_WIDTH),
                  pl.BlockSpec((1, N_MOD, d), lambda b, t: (b, 0, 0)),
                  pl.BlockSpec(wo.shape, lambda b, t: (0, 0), **const),
                  pl.BlockSpec((1, d), lambda b, t: (0, 0)),
                  pl.BlockSpec(w1.shape, lambda b, t: (0, 0), **const),
                  pl.BlockSpec(w2.shape, lambda b, t: (0, 0), **const),
                  pl.BlockSpec((1, d), lambda b, t: (0, 0))],
        out_specs=tok(d),
        out_shape=jax.ShapeDtypeStruct((bsz, seq, d), jnp.float32),
        compiler_params=pltpu.CompilerParams(dimension_semantics=("arbitrary", "arbitrary"),
                                             vmem_limit_bytes=VMEM_LIMIT),
        name="out_ffn",
    )(x, o_diff, o_sb, mod3, wo, gf.reshape(1, d), w1, w2, gl.reshape(1, d))


def _rope_tables(seq_len):
    dim = HEAD_DIM
    inv = 1.0 / (ROPE_THETA ** (jnp.arange(0, dim, 2, dtype=jnp.float32) / dim))
    ang = jnp.arange(seq_len, dtype=jnp.float32)[:, None] * inv[None, :]
    ang = jnp.concatenate([ang, ang], axis=-1)
    cos, sin = jnp.cos(ang), jnp.sin(ang)
    sign = jnp.where(jnp.arange(dim) < dim // 2, -1.0, 1.0).astype(jnp.float32)
    reps = LANES // dim
    return jnp.tile(cos, (1, reps)), jnp.tile(sin * sign[None, :], (1, reps))


def kernel(x, c, ada_w, ada_b, mix_norm_g, w_in, lambda_q1, lambda_k1, lambda_q2, lambda_k2,
           diff_subln_g, w_out, ffn_norm_g, w_ff1, w_ff2, final_norm_g):
    bsz, seq, d = x.shape
    depth = ada_w.shape[0]
    assert depth == 1, "the fused final norm assumes a single layer"
    assert seq % TOKEN_TILE == 0 and seq % ATT_BLOCK == 0
    cos128, sin128 = _rope_tables(seq)
    bf = jnp.bfloat16
    layer = 0
    lambda_init = 0.8 - 0.6 * math.exp(-0.3 * layer)
    mod3 = _adaln_call(c, ada_w[layer], ada_b[layer]).reshape(bsz, N_MOD, d)
    proj = _inproj_call(x, mod3, mix_norm_g[layer], w_in[layer].astype(bf), cos128, sin128)
    proj4 = proj.reshape(bsz, seq // ATT_BLOCK, ATT_BLOCK, proj.shape[-1])
    o_diff = _diff_attn_call(proj4, lambda_q1[layer], lambda_k1[layer], lambda_q2[layer],
                             lambda_k2[layer], diff_subln_g[layer], lambda_init)
    o_sb = _sb_attn_call(proj4)
    return _out_ffn_call(x, o_diff, o_sb, mod3, w_out[layer].astype(bf), ffn_norm_g[layer],
                         w_ff1[layer].astype(bf), w_ff2[layer].astype(bf), final_norm_g)
```

```python
import functools
import math

import jax
import jax.numpy as jnp
from jax import lax
from jax.experimental import pallas as pl
from jax.experimental.pallas import tpu as pltpu

HEAD_DIM = 64
N_DIFF_HEADS = 4
N_SB_HEADS = 8
DIFF_WIDTH = N_DIFF_HEADS * 2 * HEAD_DIM
SB_WIDTH = N_SB_HEADS * HEAD_DIM
CHUNK = 64
ROPE_THETA = 10000.0
NORM_EPS = 1e-6
N_MOD = 6

LANES = 128
SUBLANES = 8
ATT_BLOCK = 256
SEG_LEN = ATT_BLOCK // SUBLANES
TOKEN_TILE = 512
N_ITEMS = 8
N_PAIRS = N_ITEMS // 2
PAIR_ISSUE = ((1, 2), (3,), (4,), ())
ONES_ROWS = 16
LOG2E = 1.4426950408889634
VMEM_LIMIT = 56 * 1024 * 1024

_NT = (((1,), (1,)), ((), ()))
_NN = (((1,), (0,)), ((), ()))


def _rms_scale(xf):
    return lax.rsqrt(jnp.mean(xf * xf, axis=-1, keepdims=True) + NORM_EPS)


def _adaln_kernel(c_ref, w_ref, b_ref, o_ref):
    c = c_ref[...]
    ca = c / (1.0 + jnp.exp(-c))
    o_ref[...] = lax.dot_general(ca.astype(jnp.bfloat16), w_ref[...].astype(jnp.bfloat16), _NN,
                                 preferred_element_type=jnp.float32) + b_ref[...]


def _adaln_call(c, w, b):
    bsz, d = c.shape
    n = w.shape[1]
    tn = 1024
    return pl.pallas_call(
        _adaln_kernel,
        grid=(n // tn,),
        in_specs=[pl.BlockSpec((bsz, d), lambda j: (0, 0)),
                  pl.BlockSpec((d, tn), lambda j: (0, j)),
                  pl.BlockSpec((1, tn), lambda j: (0, j))],
        out_specs=pl.BlockSpec((bsz, tn), lambda j: (0, j)),
        out_shape=jax.ShapeDtypeStruct((bsz, n), jnp.float32),
        compiler_params=pltpu.CompilerParams(dimension_semantics=("arbitrary",),
                                             vmem_limit_bytes=VMEM_LIMIT),
        name="adaln",
    )(c, w, b.reshape(1, n))


def _inproj_kernel(x_ref, mod_ref, g_ref, w_ref, cos_ref, sin_ref, o_ref):
    xf = x_ref[0]
    sh = mod_ref[0, 0:1, :]
    sc = mod_ref[0, 1:2, :]
    h = (xf * _rms_scale(xf) * g_ref[...]) * (1.0 + sc) + sh
    hb = h.astype(jnp.bfloat16)
    cos = cos_ref[...]
    sin = sin_ref[...]
    lane = lax.broadcasted_iota(jnp.int32, cos.shape, 1)
    first_half = (lane % HEAD_DIM) < (HEAD_DIM // 2)
    scale = HEAD_DIM ** -0.5
    gw = DIFF_WIDTH
    for n in range(6):
        acc = lax.dot_general(hb, w_ref[:, n * gw:(n + 1) * gw], _NN,
                              preferred_element_type=jnp.float32)
        if n < 2:
            for j in range(gw // LANES):
                xb = acc[:, j * LANES:(j + 1) * LANES]
                rot = jnp.where(first_half, pltpu.roll(xb, LANES - HEAD_DIM // 2, 1),
                                pltpu.roll(xb, HEAD_DIM // 2, 1))
                yb = xb * cos + rot * sin
                if n == 0:
                    yb = yb * (scale * LOG2E)
                o_ref[0, :, n * gw + j * LANES:n * gw + (j + 1) * LANES] = yb.astype(o_ref.dtype)
        else:
            if n == 3:
                acc = acc * (0.5 * scale)
            o_ref[0, :, n * gw:(n + 1) * gw] = acc.astype(o_ref.dtype)


def _inproj_call(x, mod3, g, w_bf16, cos128, sin128):
    bsz, seq, d = x.shape
    n = w_bf16.shape[1]
    tm = TOKEN_TILE
    const = dict(pipeline_mode=pl.Buffered(1))
    return pl.pallas_call(
        _inproj_kernel,
        grid=(bsz, seq // tm),
        in_specs=[pl.BlockSpec((1, tm, d), lambda b, t: (b, t, 0)),
                  pl.BlockSpec((1, N_MOD, d), lambda b, t: (b, 0, 0)),
                  pl.BlockSpec((1, d), lambda b, t: (0, 0)),
                  pl.BlockSpec((d, n), lambda b, t: (0, 0), **const),
                  pl.BlockSpec((tm, LANES), lambda b, t: (t, 0)),
                  pl.BlockSpec((tm, LANES), lambda b, t: (t, 0))],
        out_specs=pl.BlockSpec((1, tm, n), lambda b, t: (b, t, 0)),
        out_shape=jax.ShapeDtypeStruct((bsz, seq, n), jnp.bfloat16),
        compiler_params=pltpu.CompilerParams(dimension_semantics=("arbitrary", "arbitrary"),
                                             vmem_limit_bytes=VMEM_LIMIT),
        name="inproj",
    )(x, mod3, g.reshape(1, d), w_bf16, cos128, sin128)


def _diff_kernel(lq1_ref, lk1_ref, lq2_ref, lk2_ref, g_ref, q_ref, k_ref, v_ref, o_ref,
                 vt_scr, qm_scr, acc_scr, m_scr, pre_scr, s_scr, p_scr, alpha_scr, *, lambda_init):
    i = pl.program_id(1)
    nblk = v_ref.shape[1]
    tb = ATT_BLOCK
    tw = 2 * tb
    dv = 2 * HEAD_DIM
    vrows = dv + ONES_ROWS
    last = N_DIFF_HEADS - 1

    @pl.when(i == 0)
    def _():
        for blk in range(nblk):
            vt = v_ref[0, blk].astype(jnp.float32).T.astype(jnp.bfloat16)
            for hd in range(N_DIFF_HEADS):
                vt_scr[blk, hd * vrows:hd * vrows + dv, :] = vt[hd * dv:(hd + 1) * dv, :]
                vt_scr[blk, hd * vrows + dv:(hd + 1) * vrows, :] = jnp.ones((ONES_ROWS, tb), jnp.bfloat16)

    lane = lax.broadcasted_iota(jnp.int32, (tb, LANES), 1)
    for hd in range(N_DIFF_HEADS):
        qp = q_ref[0, 0, :, hd * LANES:(hd + 1) * LANES]
        zero = jnp.zeros_like(qp)
        qm_scr[hd, :tb, :] = jnp.where(lane < HEAD_DIM, qp, zero)
        qm_scr[hd, tb:, :] = jnp.where(lane >= HEAD_DIM, qp, zero)
    acc_scr[last] = jnp.zeros((vrows, tw), jnp.float32)
    alpha_scr[...] = jnp.zeros((1, tw), jnp.float32)

    qcol = lax.broadcasted_iota(jnp.int32, (SUBLANES, tw), 1) % tb

    def scores(j, hd):
        return lax.dot_general(k_ref[0, j, :, hd * LANES:(hd + 1) * LANES], qm_scr[hd], _NT,
                               preferred_element_type=jnp.float32)

    def elementwise(s_t, hd, first):
        buf = hd % 2
        mx = None
        for v in range(tb // SUBLANES):
            sv = s_t[v * SUBLANES:(v + 1) * SUBLANES, :]
            if first:
                sv = jnp.where(qcol >= (v * SUBLANES // CHUNK) * CHUNK, sv, -jnp.inf)
            mx = sv if mx is None else jnp.maximum(mx, sv)
            s_scr[buf, v * SUBLANES:(v + 1) * SUBLANES, :] = sv
        tmax = jnp.max(mx, axis=0, keepdims=True)
        if first:
            m_new = tmax
            alpha = None
        else:
            m_old = m_scr[hd]
            m_new = jnp.maximum(m_old, tmax)
            alpha = jnp.exp2(m_old - m_new)
        m_scr[hd] = m_new
        return jnp.exp2(s_scr[buf] - m_new).astype(jnp.bfloat16), alpha

    def pv_product(j, hd, p):
        return lax.dot_general(vt_scr[j, hd * vrows:(hd + 1) * vrows, :], p, _NN,
                               preferred_element_type=jnp.float32)

    def finish_last(j_prev):
        acc_scr[last] = alpha_scr[...] * acc_scr[last] + pv_product(j_prev, last, p_scr[...])

    def tile(j, j_next, j_prev, first):
        if not first:
            finish_last(j_prev)
        ss = [None] * N_DIFF_HEADS
        ss[0] = pre_scr[...]
        for hd in range(N_DIFF_HEADS):
            for idx in PAIR_ISSUE[hd]:
                if idx < N_DIFF_HEADS:
                    ss[idx] = scores(j, idx)
                else:
                    pre_scr[...] = scores(j_next, idx - N_DIFF_HEADS)
            p, alpha = elementwise(ss[hd], hd, first)
            if hd == last:
                p_scr[...] = p
                if not first:
                    alpha_scr[...] = alpha
            elif first:
                acc_scr[hd] = pv_product(j, hd, p)
            else:
                acc_scr[hd] = alpha * acc_scr[hd] + pv_product(j, hd, p)

    pre_scr[...] = scores(i, 0)
    tile(i, jnp.maximum(i - 1, 0), i, True)

    def body(t, c):
        j = i - 1 - t
        tile(j, jnp.maximum(j - 1, 0), j + 1, False)
        return c

    lax.fori_loop(0, i, body, 0)
    finish_last(0)

    lam = (jnp.exp(jnp.sum(lq1_ref[...] * lk1_ref[...], axis=-1, keepdims=True))
           - jnp.exp(jnp.sum(lq2_ref[...] * lk2_ref[...], axis=-1, keepdims=True)) + lambda_init)
    for hd in range(N_DIFF_HEADS):
        a1 = acc_scr[hd, :, :tb]
        a2 = acc_scr[hd, :, tb:]
        o_t = a1[:dv] / a1[dv:dv + 1] - lam * (a2[:dv] / a2[dv:dv + 1])
        ms = jnp.mean(o_t * o_t, axis=0, keepdims=True)
        o_t = o_t * lax.rsqrt(ms + NORM_EPS) * g_ref[...] * (1.0 - lambda_init)
        o_ref[0, :, hd * LANES:(hd + 1) * LANES] = o_t.T.astype(o_ref.dtype)


def _diff_attn_call(proj4, lq1, lk1, lq2, lk2, subln_g, lambda_init):
    bsz, nblk, tb, _ = proj4.shape
    seq = nblk * tb
    w = DIFF_WIDTH
    vrows = 2 * HEAD_DIM + ONES_ROWS
    vec = lambda a: a.reshape(1, HEAD_DIM)
    small = pl.BlockSpec((1, HEAD_DIM), lambda b, i: (0, 0))
    return pl.pallas_call(
        functools.partial(_diff_kernel, lambda_init=lambda_init),
        grid=(bsz, nblk),
        in_specs=[small, small, small, small,
                  pl.BlockSpec((2 * HEAD_DIM, 1), lambda b, i: (0, 0)),
                  pl.BlockSpec((1, 1, tb, w), lambda b, i: (b, i, 0, 0)),
                  pl.BlockSpec((1, nblk, tb, w), lambda b, i: (b, 0, 0, 1)),
                  pl.BlockSpec((1, nblk, tb, w), lambda b, i: (b, 0, 0, 2))],
        out_specs=pl.BlockSpec((1, tb, w), lambda b, i: (b, i, 0)),
        out_shape=jax.ShapeDtypeStruct((bsz, seq, w), jnp.bfloat16),
        scratch_shapes=[pltpu.VMEM((nblk, N_DIFF_HEADS * vrows, tb), jnp.bfloat16),
                        pltpu.VMEM((N_DIFF_HEADS, 2 * tb, LANES), jnp.bfloat16),
                        pltpu.VMEM((N_DIFF_HEADS, vrows, 2 * tb), jnp.float32),
                        pltpu.VMEM((N_DIFF_HEADS, 1, 2 * tb), jnp.float32),
                        pltpu.VMEM((tb, 2 * tb), jnp.float32),
                        pltpu.VMEM((2, tb, 2 * tb), jnp.float32),
                        pltpu.VMEM((tb, 2 * tb), jnp.bfloat16),
                        pltpu.VMEM((1, 2 * tb), jnp.float32)],
        compiler_params=pltpu.CompilerParams(
            dimension_semantics=("arbitrary", "arbitrary"), vmem_limit_bytes=VMEM_LIMIT),
        name="diff_attn",
    )(vec(lq1), vec(lk1), vec(lq2), vec(lk2), subln_g.reshape(2 * HEAD_DIM, 1), proj4, proj4, proj4)


def _sb_kernel(q_ref, k_ref, v_ref, o_ref, kp_scr, vt_scr, qm_scr, acc_scr, pre_scr, ap_scr, carry_scr, a_scr):
    i = pl.program_id(1)
    nblk = k_ref.shape[1]
    tb = ATT_BLOCK
    last = N_SB_HEADS - 1

    @pl.when(i == 0)
    def _():
        row = lax.broadcasted_iota(jnp.int32, (tb, tb), 0)
        col = lax.broadcasted_iota(jnp.int32, (tb, tb), 1)
        pos = SEG_LEN * (row % SUBLANES) + (SEG_LEN - 1 - row // SUBLANES)
        perm = jnp.where(col == pos, 1.0, 0.0).astype(jnp.bfloat16)
        for blk in range(nblk):
            kp = lax.dot_general(perm, k_ref[0, blk], _NN, preferred_element_type=jnp.float32)
            kp_scr[blk] = kp.astype(jnp.bfloat16)
            vp = lax.dot_general(perm, v_ref[0, blk], _NN, preferred_element_type=jnp.float32)
            vt_scr[blk] = vp.T.astype(jnp.bfloat16)

    lane = lax.broadcasted_iota(jnp.int32, (tb, LANES), 1)
    for pr in range(N_PAIRS):
        qp = q_ref[0, 0, :, pr * LANES:(pr + 1) * LANES]
        zero = jnp.zeros_like(qp)
        qm_scr[pr, :tb, :] = jnp.where(lane < HEAD_DIM, qp, zero)
        qm_scr[pr, tb:, :] = jnp.where(lane >= HEAD_DIM, qp, zero)
    for h in range(N_SB_HEADS):
        carry_scr[h] = jnp.ones((SUBLANES, tb), jnp.float32)
    acc_scr[last * HEAD_DIM:, :] = jnp.zeros((HEAD_DIM, tb), jnp.float32)

    sub = lax.broadcasted_iota(jnp.int32, (SUBLANES, tb), 0)
    qcol = lax.broadcasted_iota(jnp.int32, (SUBLANES, tb), 1)

    def scores(j, pair):
        return lax.dot_general(kp_scr[j, :, pair * LANES:(pair + 1) * LANES], qm_scr[pair], _NT,
                               preferred_element_type=jnp.float32)

    def elementwise(zh, h, diag):
        buf = h % 2
        rem = jnp.ones((SUBLANES, tb), jnp.float32)
        prev = None
        for v in range(SEG_LEN):
            beta = 0.5 * jnp.tanh(zh[v * SUBLANES:(v + 1) * SUBLANES, :]) + 0.5
            if diag:
                beta = jnp.where(SEG_LEN * sub + (SEG_LEN - 1 - v) < qcol, beta, 0.0)
            a_v = beta * rem
            rem = rem - a_v
            if v % 2 == 0:
                prev = a_v
            else:
                ap_scr[buf, (v - 1) * SUBLANES:(v + 1) * SUBLANES, :] = jnp.concatenate(
                    [prev, a_v], axis=0).astype(jnp.bfloat16)
        incl = rem
        for k in (1, 2, 4):
            shifted = pltpu.roll(incl, SUBLANES - k, 0)
            incl = incl * jnp.where(sub + k < SUBLANES, shifted, 1.0)
        excl = jnp.where(sub + 1 < SUBLANES, pltpu.roll(incl, SUBLANES - 1, 0), 1.0)
        carry = carry_scr[h]
        start = carry * excl
        carry_scr[h] = carry * jnp.broadcast_to(incl[0:1, :], carry.shape)
        start2 = jnp.concatenate([start, start], axis=0).astype(jnp.bfloat16)
        return jnp.concatenate(
            [ap_scr[buf, u * 2 * SUBLANES:(u + 1) * 2 * SUBLANES, :] * start2 for u in range(SEG_LEN // 2)],
            axis=0)

    def av_product(j, h, a):
        return lax.dot_general(vt_scr[j, h * HEAD_DIM:(h + 1) * HEAD_DIM, :], a, _NN,
                               preferred_element_type=jnp.float32)

    def finish_last(j_prev):
        acc_scr[last * HEAD_DIM:, :] += av_product(j_prev, last, a_scr[...])

    def tile(j, j_next, j_prev, diag):
        if not diag:
            finish_last(j_prev)
        zs = [None] * N_PAIRS
        zs[0] = pre_scr[...]
        for h in range(N_SB_HEADS):
            if h % 2 == 0:
                for idx in PAIR_ISSUE[h // 2]:
                    if idx < N_PAIRS:
                        zs[idx] = scores(j, idx)
                    else:
                        pre_scr[...] = scores(j_next, idx - N_PAIRS)
            a = elementwise(zs[h // 2][:, (h % 2) * tb:(h % 2 + 1) * tb], h, diag)
            rows = slice(h * HEAD_DIM, (h + 1) * HEAD_DIM)
            if h == last:
                a_scr[...] = a
            elif diag:
                acc_scr[rows, :] = av_product(j, h, a)
            else:
                acc_scr[rows, :] += av_product(j, h, a)

    pre_scr[...] = scores(i, 0)
    tile(i, jnp.maximum(i - 1, 0), i, True)

    def body(t, c):
        j = i - 1 - t
        tile(j, jnp.maximum(j - 1, 0), j + 1, False)
        return c

    lax.fori_loop(0, i, body, 0)
    finish_last(0)
    o_ref[0] = acc_scr[...].T.astype(o_ref.dtype)


def _sb_attn_call(proj4):
    bsz, nblk, tb, _ = proj4.shape
    seq = nblk * tb
    w = SB_WIDTH
    base = 3 * DIFF_WIDTH // w
    return pl.pallas_call(
        _sb_kernel,
        grid=(bsz, nblk),
        in_specs=[pl.BlockSpec((1, 1, tb, w), lambda b, i: (b, i, 0, base)),
                  pl.BlockSpec((1, nblk, tb, w), lambda b, i: (b, 0, 0, base + 1)),
                  pl.BlockSpec((1, nblk, tb, w), lambda b, i: (b, 0, 0, base + 2))],
        out_specs=pl.BlockSpec((1, tb, w), lambda b, i: (b, i, 0)),
        out_shape=jax.ShapeDtypeStruct((bsz, seq, w), jnp.bfloat16),
        scratch_shapes=[pltpu.VMEM((nblk, tb, w), jnp.bfloat16),
                        pltpu.VMEM((nblk, w, tb), jnp.bfloat16),
                        pltpu.VMEM((N_PAIRS, 2 * tb, LANES), jnp.bfloat16),
                        pltpu.VMEM((w, tb), jnp.float32),
                        pltpu.VMEM((tb, 2 * tb), jnp.float32),
                        pltpu.VMEM((2, tb, tb), jnp.bfloat16),
                        pltpu.VMEM((N_SB_HEADS, SUBLANES, tb), jnp.float32),
                        pltpu.VMEM((tb, tb), jnp.bfloat16)],
        compiler_params=pltpu.CompilerParams(
            dimension_semantics=("arbitrary", "arbitrary"), vmem_limit_bytes=VMEM_LIMIT),
        name="sb_attn",
    )(proj4, proj4, proj4)


def _out_ffn_kernel(x_ref, od_ref, os_ref, mod_ref, wo_ref, gf_ref, w1_ref, w2_ref, gl_ref, o_ref):
    g_m = mod_ref[0, 2:3, :]
    sh_f = mod_ref[0, 3:4, :]
    sc_f = mod_ref[0, 4:5, :]
    g_f = mod_ref[0, 5:6, :]
    mixed = (lax.dot_general(od_ref[0], wo_ref[:DIFF_WIDTH, :], _NN, preferred_element_type=jnp.float32)
             + lax.dot_general(os_ref[0], wo_ref[DIFF_WIDTH:, :], _NN, preferred_element_type=jnp.float32))
    x1 = x_ref[0] + g_m * mixed
    h = ((x1 * _rms_scale(x1) * gf_ref[...]) * (1.0 + sc_f) + sh_f).astype(jnp.bfloat16)
    d_ff = w1_ref.shape[1]
    fc = 1024
    f = None
    for c in range(d_ff // fc):
        u = lax.dot_general(h, w1_ref[:, c * fc:(c + 1) * fc], _NN, preferred_element_type=jnp.float32)
        r = jnp.square(jnp.maximum(u, 0.0)).astype(jnp.bfloat16)
        part = lax.dot_general(r, w2_ref[c * fc:(c + 1) * fc, :], _NN, preferred_element_type=jnp.float32)
        f = part if f is None else f + part
    x2 = x1 + g_f * f
    o_ref[0] = x2 * _rms_scale(x2) * gl_ref[...]


def _out_ffn_call(x, o_diff, o_sb, mod3, wo, gf, w1, w2, gl):
    bsz, seq, d = x.shape
    tm = TOKEN_TILE
    const = dict(pipeline_mode=pl.Buffered(1))
    tok = lambda w: pl.BlockSpec((1, tm, w), lambda b, t: (b, t, 0))
    return pl.pallas_call(
        _out_ffn_kernel,
        grid=(bsz, seq // tm),
        in_specs=[tok(d), tok(DIFF_WIDTH), tok(SB_WIDTH),
                  pl.BlockSpec((1, N_MOD, d), lambda b, t: (b, 0, 0)),
                  pl.BlockSpec(wo.shape, lambda b, t: (0, 0), **const),
                  pl.BlockSpec((1, d), lambda b, t: (0, 0)),
                  pl.BlockSpec(w1.shape, lambda b, t: (0, 0), **const),
                  pl.BlockSpec(w2.shape, lambda b, t: (0, 0), **const),
                  pl.BlockSpec((1, d), lambda b, t: (0, 0))],
        out_specs=tok(d),
        out_shape=jax.ShapeDtypeStruct((bsz, seq, d), jnp.float32),
        compiler_params=pltpu.CompilerParams(dimension_semantics=("arbitrary", "arbitrary"),
                                             vmem_limit_bytes=VMEM_LIMIT),
        name="out_ffn",
    )(x, o_diff, o_sb, mod3, wo, gf.reshape(1, d), w1, w2, gl.reshape(1, d))


def _rope_tables(seq_len):
    dim = HEAD_DIM
    inv = 1.0 / (ROPE_THETA ** (jnp.arange(0, dim, 2, dtype=jnp.float32) / dim))
    ang = jnp.arange(seq_len, dtype=jnp.float32)[:, None] * inv[None, :]
    ang = jnp.concatenate([ang, ang], axis=-1)
    cos, sin = jnp.cos(ang), jnp.sin(ang)
    sign = jnp.where(jnp.arange(dim) < dim // 2, -1.0, 1.0).astype(jnp.float32)
    reps = LANES // dim
    return jnp.tile(cos, (1, reps)), jnp.tile(sin * sign[None, :], (1, reps))


def kernel(x, c, ada_w, ada_b, mix_norm_g, w_in, lambda_q1, lambda_k1, lambda_q2, lambda_k2,
           diff_subln_g, w_out, ffn_norm_g, w_ff1, w_ff2, final_norm_g):
    bsz, seq, d = x.shape
    depth = ada_w.shape[0]
    assert depth == 1, "the fused final norm assumes a single layer"
    assert seq % TOKEN_TILE == 0 and seq % ATT_BLOCK == 0
    cos128, sin128 = _rope_tables(seq)
    bf = jnp.bfloat16
    layer = 0
    lambda_init = 0.8 - 0.6 * math.exp(-0.3 * layer)
    mod3 = _adaln_call(c, ada_w[layer], ada_b[layer]).reshape(bsz, N_MOD, d)
    proj = _inproj_call(x, mod3, mix_norm_g[layer], w_in[layer].astype(bf), cos128, sin128)
    proj4 = proj.reshape(bsz, seq // ATT_BLOCK, ATT_BLOCK, proj.shape[-1])
    o_diff = _diff_attn_call(proj4, lambda_q1[layer], lambda_k1[layer], lambda_q2[layer],
                             lambda_k2[layer], diff_subln_g[layer], lambda_init)
    o_sb = _sb_attn_call(proj4)
    return _out_ffn_call(x, o_diff, o_sb, mod3, w_out[layer].astype(bf), ffn_norm_g[layer],
                         w_ff1[layer].astype(bf), w_ff2[layer].astype(bf), final_norm_g)
```

```python
import functools
import math

import jax
import jax.numpy as jnp
from jax import lax
from jax.experimental import pallas as pl
from jax.experimental.pallas import tpu as pltpu

HEAD_DIM = 64
N_DIFF_HEADS = 4
N_SB_HEADS = 8
DIFF_WIDTH = N_DIFF_HEADS * 2 * HEAD_DIM
SB_WIDTH = N_SB_HEADS * HEAD_DIM
CHUNK = 64
ROPE_THETA = 10000.0
NORM_EPS = 1e-6
N_MOD = 6

LANES = 128
SUBLANES = 8
ATT_BLOCK = 256
SEG_LEN = ATT_BLOCK // SUBLANES
TOKEN_TILE = 512
N_ITEMS = 8
N_PAIRS = N_ITEMS // 2
Q_GROUP = 2
ONES_ROWS = 16
LOG2E = 1.4426950408889634
VMEM_LIMIT = 56 * 1024 * 1024

_NT = (((1,), (1,)), ((), ()))
_NN = (((1,), (0,)), ((), ()))


def _rms_scale(xf):
    return lax.rsqrt(jnp.mean(xf * xf, axis=-1, keepdims=True) + NORM_EPS)


def _adaln_kernel(c_ref, w_ref, b_ref, o_ref):
    c = c_ref[...]
    ca = c / (1.0 + jnp.exp(-c))
    o_ref[...] = lax.dot_general(ca.astype(jnp.bfloat16), w_ref[...].astype(jnp.bfloat16), _NN,
                                 preferred_element_type=jnp.float32) + b_ref[...]


def _adaln_call(c, w, b):
    bsz, d = c.shape
    n = w.shape[1]
    tn = 1024
    return pl.pallas_call(
        _adaln_kernel,
        grid=(n // tn,),
        in_specs=[pl.BlockSpec((bsz, d), lambda j: (0, 0)),
                  pl.BlockSpec((d, tn), lambda j: (0, j)),
                  pl.BlockSpec((1, tn), lambda j: (0, j))],
        out_specs=pl.BlockSpec((bsz, tn), lambda j: (0, j)),
        out_shape=jax.ShapeDtypeStruct((bsz, n), jnp.float32),
        compiler_params=pltpu.CompilerParams(dimension_semantics=("arbitrary",),
                                             vmem_limit_bytes=VMEM_LIMIT),
        name="adaln",
    )(c, w, b.reshape(1, n))


def _inproj_kernel(x_ref, mod_ref, g_ref, w_ref, cos_ref, sin_ref, o_ref):
    xf = x_ref[0]
    sh = mod_ref[0, 0:1, :]
    sc = mod_ref[0, 1:2, :]
    h = (xf * _rms_scale(xf) * g_ref[...]) * (1.0 + sc) + sh
    hb = h.astype(jnp.bfloat16)
    cos = cos_ref[...]
    sin = sin_ref[...]
    lane = lax.broadcasted_iota(jnp.int32, cos.shape, 1)
    first_half = (lane % HEAD_DIM) < (HEAD_DIM // 2)
    scale = HEAD_DIM ** -0.5
    gw = DIFF_WIDTH
    for n in range(6):
        acc = lax.dot_general(hb, w_ref[:, n * gw:(n + 1) * gw], _NN,
                              preferred_element_type=jnp.float32)
        if n < 2:
            for j in range(gw // LANES):
                xb = acc[:, j * LANES:(j + 1) * LANES]
                rot = jnp.where(first_half, pltpu.roll(xb, LANES - HEAD_DIM // 2, 1),
                                pltpu.roll(xb, HEAD_DIM // 2, 1))
                yb = xb * cos + rot * sin
                if n == 0:
                    yb = yb * (scale * LOG2E)
                o_ref[0, :, n * gw + j * LANES:n * gw + (j + 1) * LANES] = yb.astype(o_ref.dtype)
        else:
            if n == 3:
                acc = acc * (0.5 * scale)
            o_ref[0, :, n * gw:(n + 1) * gw] = acc.astype(o_ref.dtype)


def _inproj_call(x, mod3, g, w_bf16, cos128, sin128):
    bsz, seq, d = x.shape
    n = w_bf16.shape[1]
    tm = TOKEN_TILE
    const = dict(pipeline_mode=pl.Buffered(1))
    return pl.pallas_call(
        _inproj_kernel,
        grid=(bsz, seq // tm),
        in_specs=[pl.BlockSpec((1, tm, d), lambda b, t: (b, t, 0)),
                  pl.BlockSpec((1, N_MOD, d), lambda b, t: (b, 0, 0)),
                  pl.BlockSpec((1, d), lambda b, t: (0, 0)),
                  pl.BlockSpec((d, n), lambda b, t: (0, 0), **const),
                  pl.BlockSpec((tm, LANES), lambda b, t: (t, 0)),
                  pl.BlockSpec((tm, LANES), lambda b, t: (t, 0))],
        out_specs=pl.BlockSpec((1, tm, n), lambda b, t: (b, t, 0)),
        out_shape=jax.ShapeDtypeStruct((bsz, seq, n), jnp.bfloat16),
        compiler_params=pltpu.CompilerParams(dimension_semantics=("arbitrary", "arbitrary"),
                                             vmem_limit_bytes=VMEM_LIMIT),
        name="inproj",
    )(x, mod3, g.reshape(1, d), w_bf16, cos128, sin128)


def _diff_kernel(lq1_ref, lk1_ref, lq2_ref, lk2_ref, g_ref, q_ref, k_ref, v_ref, o_ref,
                 vt_scr, qm_scr, acc_scr, m_scr, pre_scr, s_scr, p_scr, alpha_scr, *, lambda_init):
    g = pl.program_id(1)
    nblk = v_ref.shape[1]
    tb = ATT_BLOCK
    tw = 2 * tb
    dv = 2 * HEAD_DIM
    vrows = dv + ONES_ROWS
    nh = N_DIFF_HEADS
    last_item = (Q_GROUP - 1, nh - 1)

    @pl.when(g == 0)
    def _():
        for blk in range(nblk):
            vt = v_ref[0, blk].astype(jnp.float32).T.astype(jnp.bfloat16)
            for hd in range(nh):
                vt_scr[blk, hd * vrows:hd * vrows + dv, :] = vt[hd * dv:(hd + 1) * dv, :]
                vt_scr[blk, hd * vrows + dv:(hd + 1) * vrows, :] = jnp.ones((ONES_ROWS, tb), jnp.bfloat16)

    lane = lax.broadcasted_iota(jnp.int32, (tb, LANES), 1)
    for qb in range(Q_GROUP):
        for hd in range(nh):
            qp = q_ref[0, qb, :, hd * LANES:(hd + 1) * LANES]
            zero = jnp.zeros_like(qp)
            qm_scr[qb * nh + hd, :tb, :] = jnp.where(lane < HEAD_DIM, qp, zero)
            qm_scr[qb * nh + hd, tb:, :] = jnp.where(lane >= HEAD_DIM, qp, zero)
    acc_scr[Q_GROUP * nh - 1] = jnp.zeros((vrows, tw), jnp.float32)
    alpha_scr[...] = jnp.zeros((1, tw), jnp.float32)

    qcol = lax.broadcasted_iota(jnp.int32, (SUBLANES, tw), 1) % tb

    def scores(j, qb, hd):
        return lax.dot_general(k_ref[0, j, :, hd * LANES:(hd + 1) * LANES], qm_scr[qb * nh + hd], _NT,
                               preferred_element_type=jnp.float32)

    def elementwise(s_t, qb, hd, first):
        buf = hd % 2
        it = qb * nh + hd
        mx = None
        for v in range(tb // SUBLANES):
            sv = s_t[v * SUBLANES:(v + 1) * SUBLANES, :]
            if first:
                sv = jnp.where(qcol >= (v * SUBLANES // CHUNK) * CHUNK, sv, -jnp.inf)
            mx = sv if mx is None else jnp.maximum(mx, sv)
            s_scr[buf, v * SUBLANES:(v + 1) * SUBLANES, :] = sv
        tmax = jnp.max(mx, axis=0, keepdims=True)
        if first:
            m_new = tmax
            alpha = None
        else:
            m_old = m_scr[it]
            m_new = jnp.maximum(m_old, tmax)
            alpha = jnp.exp2(m_old - m_new)
        m_scr[it] = m_new
        return jnp.exp2(s_scr[buf] - m_new).astype(jnp.bfloat16), alpha

    def pv_product(j, hd, p):
        return lax.dot_general(vt_scr[j, hd * vrows:(hd + 1) * vrows, :], p, _NN,
                               preferred_element_type=jnp.float32)

    def finish_deferred(j_prev):
        it = Q_GROUP * nh - 1
        acc_scr[it] = alpha_scr[...] * acc_scr[it] + pv_product(j_prev, nh - 1, p_scr[...])

    def tile_body(j, blocks, nxt, j_prev):
        if j_prev is not None:
            finish_deferred(j_prev)
        items = [(qb, hd, masked) for qb, masked in blocks for hd in range(nh)]
        ss = [None] * len(items)
        ss[0] = pre_scr[...]
        issued = 1

        def issue(upto):
            nonlocal issued
            while issued <= upto:
                if issued < len(items):
                    ss[issued] = scores(j, items[issued][0], items[issued][1])
                elif issued == len(items):
                    pre_scr[...] = scores(nxt[0], nxt[1], 0)
                issued += 1

        for k, (qb, hd, masked) in enumerate(items):
            issue(k + 2)
            p, alpha = elementwise(ss[k], qb, hd, masked)
            ss[k] = None
            it = qb * nh + hd
            if (qb, hd) == last_item:
                p_scr[...] = p
                if not masked:
                    alpha_scr[...] = alpha
            elif masked:
                acc_scr[it] = pv_product(j, hd, p)
            else:
                acc_scr[it] = alpha * acc_scr[it] + pv_product(j, hd, p)

    base = g * Q_GROUP
    pre_scr[...] = scores(base + Q_GROUP - 1, Q_GROUP - 1, 0)
    j_prev = None
    for d in range(Q_GROUP - 1, -1, -1):
        blocks = [(d, True)] + [(qb, False) for qb in range(d + 1, Q_GROUP)]
        nxt = (base + d - 1, d - 1) if d > 0 else (jnp.maximum(base - 1, 0), 0)
        tile_body(base + d, blocks, nxt, j_prev)
        j_prev = base + d

    def body(t, c):
        j = base - 1 - t
        tile_body(j, [(qb, False) for qb in range(Q_GROUP)], (jnp.maximum(j - 1, 0), 0), j + 1)
        return c

    lax.fori_loop(0, base, body, 0)
    finish_deferred(0)

    lam = (jnp.exp(jnp.sum(lq1_ref[...] * lk1_ref[...], axis=-1, keepdims=True))
           - jnp.exp(jnp.sum(lq2_ref[...] * lk2_ref[...], axis=-1, keepdims=True)) + lambda_init)
    for qb in range(Q_GROUP):
        for hd in range(nh):
            a1 = acc_scr[qb * nh + hd, :, :tb]
            a2 = acc_scr[qb * nh + hd, :, tb:]
            o_t = a1[:dv] / a1[dv:dv + 1] - lam * (a2[:dv] / a2[dv:dv + 1])
            ms = jnp.mean(o_t * o_t, axis=0, keepdims=True)
            o_t = o_t * lax.rsqrt(ms + NORM_EPS) * g_ref[...] * (1.0 - lambda_init)
            o_ref[0, qb * tb:(qb + 1) * tb, hd * LANES:(hd + 1) * LANES] = o_t.T.astype(o_ref.dtype)


def _diff_attn_call(proj4, lq1, lk1, lq2, lk2, subln_g, lambda_init):
    bsz, nblk, tb, _ = proj4.shape
    seq = nblk * tb
    w = DIFF_WIDTH
    vrows = 2 * HEAD_DIM + ONES_ROWS
    nit = Q_GROUP * N_DIFF_HEADS
    assert nblk % Q_GROUP == 0
    vec = lambda a: a.reshape(1, HEAD_DIM)
    small = pl.BlockSpec((1, HEAD_DIM), lambda b, g: (0, 0))
    return pl.pallas_call(
        functools.partial(_diff_kernel, lambda_init=lambda_init),
        grid=(bsz, nblk // Q_GROUP),
        in_specs=[small, small, small, small,
                  pl.BlockSpec((2 * HEAD_DIM, 1), lambda b, g: (0, 0)),
                  pl.BlockSpec((1, Q_GROUP, tb, w), lambda b, g: (b, g, 0, 0)),
                  pl.BlockSpec((1, nblk, tb, w), lambda b, g: (b, 0, 0, 1)),
                  pl.BlockSpec((1, nblk, tb, w), lambda b, g: (b, 0, 0, 2))],
        out_specs=pl.BlockSpec((1, Q_GROUP * tb, w), lambda b, g: (b, g, 0)),
        out_shape=jax.ShapeDtypeStruct((bsz, seq, w), jnp.bfloat16),
        scratch_shapes=[pltpu.VMEM((nblk, N_DIFF_HEADS * vrows, tb), jnp.bfloat16),
                        pltpu.VMEM((nit, 2 * tb, LANES), jnp.bfloat16),
                        pltpu.VMEM((nit, vrows, 2 * tb), jnp.float32),
                        pltpu.VMEM((nit, 1, 2 * tb), jnp.float32),
                        pltpu.VMEM((tb, 2 * tb), jnp.float32),
                        pltpu.VMEM((2, tb, 2 * tb), jnp.float32),
                        pltpu.VMEM((tb, 2 * tb), jnp.bfloat16),
                        pltpu.VMEM((1, 2 * tb), jnp.float32)],
        compiler_params=pltpu.CompilerParams(
            dimension_semantics=("arbitrary", "arbitrary"), vmem_limit_bytes=VMEM_LIMIT),
        name="diff_attn",
    )(vec(lq1), vec(lk1), vec(lq2), vec(lk2), subln_g.reshape(2 * HEAD_DIM, 1), proj4, proj4, proj4)


def _sb_kernel(q_ref, k_ref, v_ref, o_ref, kp_scr, vt_scr, qm_scr, acc_scr, pre_scr, ap_scr, carry_scr, a_scr):
    g = pl.program_id(1)
    nblk = k_ref.shape[1]
    tb = ATT_BLOCK
    nh = N_SB_HEADS
    last_item = (Q_GROUP - 1, nh - 1)

    @pl.when(g == 0)
    def _():
        row = lax.broadcasted_iota(jnp.int32, (tb, tb), 0)
        col = lax.broadcasted_iota(jnp.int32, (tb, tb), 1)
        pos = SEG_LEN * (row % SUBLANES) + (SEG_LEN - 1 - row // SUBLANES)
        perm = jnp.where(col == pos, 1.0, 0.0).astype(jnp.bfloat16)
        for blk in range(nblk):
            kp = lax.dot_general(perm, k_ref[0, blk], _NN, preferred_element_type=jnp.float32)
            kp_scr[blk] = kp.astype(jnp.bfloat16)
            vp = lax.dot_general(perm, v_ref[0, blk], _NN, preferred_element_type=jnp.float32)
            vt_scr[blk] = vp.T.astype(jnp.bfloat16)

    lane = lax.broadcasted_iota(jnp.int32, (tb, LANES), 1)
    for qb in range(Q_GROUP):
        for pr in range(N_PAIRS):
            qp = q_ref[0, qb, :, pr * LANES:(pr + 1) * LANES]
            zero = jnp.zeros_like(qp)
            qm_scr[qb * N_PAIRS + pr, :tb, :] = jnp.where(lane < HEAD_DIM, qp, zero)
            qm_scr[qb * N_PAIRS + pr, tb:, :] = jnp.where(lane >= HEAD_DIM, qp, zero)
        for h in range(nh):
            carry_scr[qb * nh + h] = jnp.ones((SUBLANES, tb), jnp.float32)
    acc_scr[last_item[0], last_item[1] * HEAD_DIM:, :] = jnp.zeros((HEAD_DIM, tb), jnp.float32)

    sub = lax.broadcasted_iota(jnp.int32, (SUBLANES, tb), 0)
    qcol = lax.broadcasted_iota(jnp.int32, (SUBLANES, tb), 1)

    def scores(j, qb, pair):
        return lax.dot_general(kp_scr[j, :, pair * LANES:(pair + 1) * LANES], qm_scr[qb * N_PAIRS + pair], _NT,
                               preferred_element_type=jnp.float32)

    def elementwise(zh, qb, h, diag):
        buf = h % 2
        rem = jnp.ones((SUBLANES, tb), jnp.float32)
        prev = None
        for v in range(SEG_LEN):
            beta = 0.5 * jnp.tanh(zh[v * SUBLANES:(v + 1) * SUBLANES, :]) + 0.5
            if diag:
                beta = jnp.where(SEG_LEN * sub + (SEG_LEN - 1 - v) < qcol, beta, 0.0)
            a_v = beta * rem
            rem = rem - a_v
            if v % 2 == 0:
                prev = a_v
            else:
                ap_scr[buf, (v - 1) * SUBLANES:(v + 1) * SUBLANES, :] = jnp.concatenate(
                    [prev, a_v], axis=0).astype(jnp.bfloat16)
        incl = rem
        for k in (1, 2, 4):
            shifted = pltpu.roll(incl, SUBLANES - k, 0)
            incl = incl * jnp.where(sub + k < SUBLANES, shifted, 1.0)
        excl = jnp.where(sub + 1 < SUBLANES, pltpu.roll(incl, SUBLANES - 1, 0), 1.0)
        carry = carry_scr[qb * nh + h]
        start = carry * excl
        carry_scr[qb * nh + h] = carry * jnp.broadcast_to(incl[0:1, :], carry.shape)
        start2 = jnp.concatenate([start, start], axis=0).astype(jnp.bfloat16)
        return jnp.concatenate(
            [ap_scr[buf, u * 2 * SUBLANES:(u + 1) * 2 * SUBLANES, :] * start2 for u in range(SEG_LEN // 2)],
            axis=0)

    def av_product(j, h, a):
        return lax.dot_general(vt_scr[j, h * HEAD_DIM:(h + 1) * HEAD_DIM, :], a, _NN,
                               preferred_element_type=jnp.float32)

    def finish_deferred(j_prev):
        qb, h = last_item
        acc_scr[qb, h * HEAD_DIM:, :] += av_product(j_prev, h, a_scr[...])

    def tile_body(j, blocks, nxt, j_prev):
        if j_prev is not None:
            finish_deferred(j_prev)
        pairs = [(qb, pr, masked) for qb, masked in blocks for pr in range(N_PAIRS)]
        zs = [None] * len(pairs)
        zs[0] = pre_scr[...]
        issued = 1

        def issue(upto):
            nonlocal issued
            while issued <= upto:
                if issued < len(pairs):
                    zs[issued] = scores(j, pairs[issued][0], pairs[issued][1])
                elif issued == len(pairs):
                    pre_scr[...] = scores(nxt[0], nxt[1], 0)
                issued += 1

        for k, (qb, pr, masked) in enumerate(pairs):
            issue(k + 2)
            for hh in range(2):
                h = 2 * pr + hh
                a = elementwise(zs[k][:, hh * tb:(hh + 1) * tb], qb, h, masked)
                rows = slice(h * HEAD_DIM, (h + 1) * HEAD_DIM)
                first_write = masked
                if (qb, h) == last_item:
                    a_scr[...] = a
                elif first_write:
                    acc_scr[qb, rows, :] = av_product(j, h, a)
                else:
                    acc_scr[qb, rows, :] += av_product(j, h, a)
            zs[k] = None

    base = g * Q_GROUP
    pre_scr[...] = scores(base + Q_GROUP - 1, Q_GROUP - 1, 0)
    j_prev = None
    for d in range(Q_GROUP - 1, -1, -1):
        blocks = [(d, True)] + [(qb, False) for qb in range(d + 1, Q_GROUP)]
        nxt = (base + d - 1, d - 1) if d > 0 else (jnp.maximum(base - 1, 0), 0)
        tile_body(base + d, blocks, nxt, j_prev)
        j_prev = base + d

    def body(t, c):
        j = base - 1 - t
        tile_body(j, [(qb, False) for qb in range(Q_GROUP)], (jnp.maximum(j - 1, 0), 0), j + 1)
        return c

    lax.fori_loop(0, base, body, 0)
    finish_deferred(0)
    for qb in range(Q_GROUP):
        o_ref[0, qb * tb:(qb + 1) * tb, :] = acc_scr[qb].T.astype(o_ref.dtype)


def _sb_attn_call(proj4):
    bsz, nblk, tb, _ = proj4.shape
    seq = nblk * tb
    w = SB_WIDTH
    base = 3 * DIFF_WIDTH // w
    assert nblk % Q_GROUP == 0
    return pl.pallas_call(
        _sb_kernel,
        grid=(bsz, nblk // Q_GROUP),
        in_specs=[pl.BlockSpec((1, Q_GROUP, tb, w), lambda b, g: (b, g, 0, base)),
                  pl.BlockSpec((1, nblk, tb, w), lambda b, g: (b, 0, 0, base + 1)),
                  pl.BlockSpec((1, nblk, tb, w), lambda b, g: (b, 0, 0, base + 2))],
        out_specs=pl.BlockSpec((1, Q_GROUP * tb, w), lambda b, g: (b, g, 0)),
        out_shape=jax.ShapeDtypeStruct((bsz, seq, w), jnp.bfloat16),
        scratch_shapes=[pltpu.VMEM((nblk, tb, w), jnp.bfloat16),
                        pltpu.VMEM((nblk, w, tb), jnp.bfloat16),
                        pltpu.VMEM((Q_GROUP * N_PAIRS, 2 * tb, LANES), jnp.bfloat16),
                        pltpu.VMEM((Q_GROUP, w, tb), jnp.float32),
                        pltpu.VMEM((tb, 2 * tb), jnp.float32),
                        pltpu.VMEM((2, tb, tb), jnp.bfloat16),
                        pltpu.VMEM((Q_GROUP * N_SB_HEADS, SUBLANES, tb), jnp.float32),
                        pltpu.VMEM((tb, tb), jnp.bfloat16)],
        compiler_params=pltpu.CompilerParams(
            dimension_semantics=("arbitrary", "arbitrary"), vmem_limit_bytes=VMEM_LIMIT),
        name="sb_attn",
    )(proj4, proj4, proj4)


def _out_ffn_kernel(x_ref, od_ref, os_ref, mod_ref, wo_ref, gf_ref, w1_ref, w2_ref, gl_ref, o_ref):
    g_m = mod_ref[0, 2:3, :]
    sh_f = mod_ref[0, 3:4, :]
    sc_f = mod_ref[0, 4:5, :]
    g_f = mod_ref[0, 5:6, :]
    mixed = (lax.dot_general(od_ref[0], wo_ref[:DIFF_WIDTH, :], _NN, preferred_element_type=jnp.float32)
             + lax.dot_general(os_ref[0], wo_ref[DIFF_WIDTH:, :], _NN, preferred_element_type=jnp.float32))
    x1 = x_ref[0] + g_m * mixed
    h = ((x1 * _rms_scale(x1) * gf_ref[...]) * (1.0 + sc_f) + sh_f).astype(jnp.bfloat16)
    d_ff = w1_ref.shape[1]
    fc = 1024
    f = None
    for c in range(d_ff // fc):
        u = lax.dot_general(h, w1_ref[:, c * fc:(c + 1) * fc], _NN, preferred_element_type=jnp.float32)
        r = jnp.square(jnp.maximum(u, 0.0)).astype(jnp.bfloat16)
        part = lax.dot_general(r, w2_ref[c * fc:(c + 1) * fc, :], _NN, preferred_element_type=jnp.float32)
        f = part if f is None else f + part
    x2 = x1 + g_f * f
    o_ref[0] = x2 * _rms_scale(x2) * gl_ref[...]


def _out_ffn_call(x, o_diff, o_sb, mod3, wo, gf, w1, w2, gl):
    bsz, seq, d = x.shape
    tm = TOKEN_TILE
    const = dict(pipeline_mode=pl.Buffered(1))
    tok = lambda w: pl.BlockSpec((1, tm, w), lambda b, t: (b, t, 0))
    return pl.pallas_call(
        _out_ffn_kernel,
        grid=(bsz, seq // tm),
        in_specs=[tok(d), tok(DIFF_WIDTH), tok(SB_WIDTH),
                  pl.BlockSpec((1, N_MOD, d), lambda b, t: (b, 0, 0)),
                  pl.BlockSpec(wo.shape, lambda b, t: (0, 0), **const),
                  pl.BlockSpec((1, d), lambda b, t: (0, 0)),
                  pl.BlockSpec(w1.shape, lambda b, t: (0, 0), **const),
                  pl.BlockSpec(w2.shape, lambda b, t: (0, 0), **const),
                  pl.BlockSpec((1, d), lambda b, t: (0, 0))],
        out_specs=tok(d),
        out_shape=jax.ShapeDtypeStruct((bsz, seq, d), jnp.float32),
        compiler_params=pltpu.CompilerParams(dimension_semantics=("arbitrary", "arbitrary"),
                                             vmem_limit_bytes=VMEM_LIMIT),
        name="out_ffn",
    )(x, o_diff, o_sb, mod3, wo, gf.reshape(1, d), w1, w2, gl.reshape(1, d))


def _rope_tables(seq_len):
    dim = HEAD_DIM
    inv = 1.0 / (ROPE_THETA ** (jnp.arange(0, dim, 2, dtype=jnp.float32) / dim))
    ang = jnp.arange(seq_len, dtype=jnp.float32)[:, None] * inv[None, :]
    ang = jnp.concatenate([ang, ang], axis=-1)
    cos, sin = jnp.cos(ang), jnp.sin(ang)
    sign = jnp.where(jnp.arange(dim) < dim // 2, -1.0, 1.0).astype(jnp.float32)
    reps = LANES // dim
    return jnp.tile(cos, (1, reps)), jnp.tile(sin * sign[None, :], (1, reps))


def kernel(x, c, ada_w, ada_b, mix_norm_g, w_in, lambda_q1, lambda_k1, lambda_q2, lambda_k2,
           diff_subln_g, w_out, ffn_norm_g, w_ff1, w_ff2, final_norm_g):
    bsz, seq, d = x.shape
    depth = ada_w.shape[0]
    assert depth == 1, "the fused final norm assumes a single layer"
    assert seq % TOKEN_TILE == 0 and seq % ATT_BLOCK == 0
    cos128, sin128 = _rope_tables(seq)
    bf = jnp.bfloat16
    layer = 0
    lambda_init = 0.8 - 0.6 * math.exp(-0.3 * layer)
    mod3 = _adaln_call(c, ada_w[layer], ada_b[layer]).reshape(bsz, N_MOD, d)
    proj = _inproj_call(x, mod3, mix_norm_g[layer], w_in[layer].astype(bf), cos128, sin128)
    proj4 = proj.reshape(bsz, seq // ATT_BLOCK, ATT_BLOCK, proj.shape[-1])
    o_diff = _diff_attn_call(proj4, lambda_q1[layer], lambda_k1[layer], lambda_q2[layer],
                             lambda_k2[layer], diff_subln_g[layer], lambda_init)
    o_sb = _sb_attn_call(proj4)
    return _out_ffn_call(x, o_diff, o_sb, mod3, w_out[layer].astype(bf), ffn_norm_g[layer],
                         w_ff1[layer].astype(bf), w_ff2[layer].astype(bf), final_norm_g)
```

```python
import functools
import math

import jax
import jax.numpy as jnp
from jax import lax
from jax.experimental import pallas as pl
from jax.experimental.pallas import tpu as pltpu

HEAD_DIM = 64
N_DIFF_HEADS = 4
N_SB_HEADS = 8
DIFF_WIDTH = N_DIFF_HEADS * 2 * HEAD_DIM
SB_WIDTH = N_SB_HEADS * HEAD_DIM
CHUNK = 64
ROPE_THETA = 10000.0
NORM_EPS = 1e-6
N_MOD = 6

LANES = 128
SUBLANES = 8
ATT_BLOCK = 256
SEG_LEN = ATT_BLOCK // SUBLANES
TOKEN_TILE = 512
N_ITEMS = 8
N_PAIRS = N_ITEMS // 2
Q_GROUP = 4
ONES_ROWS = 16
LOG2E = 1.4426950408889634
VMEM_LIMIT = 56 * 1024 * 1024

_NT = (((1,), (1,)), ((), ()))
_NN = (((1,), (0,)), ((), ()))


def _rms_scale(xf):
    return lax.rsqrt(jnp.mean(xf * xf, axis=-1, keepdims=True) + NORM_EPS)


def _adaln_kernel(c_ref, w_ref, b_ref, o_ref):
    c = c_ref[...]
    ca = c / (1.0 + jnp.exp(-c))
    o_ref[...] = lax.dot_general(ca.astype(jnp.bfloat16), w_ref[...].astype(jnp.bfloat16), _NN,
                                 preferred_element_type=jnp.float32) + b_ref[...]


def _adaln_call(c, w, b):
    bsz, d = c.shape
    n = w.shape[1]
    tn = 1024
    return pl.pallas_call(
        _adaln_kernel,
        grid=(n // tn,),
        in_specs=[pl.BlockSpec((bsz, d), lambda j: (0, 0)),
                  pl.BlockSpec((d, tn), lambda j: (0, j)),
                  pl.BlockSpec((1, tn), lambda j: (0, j))],
        out_specs=pl.BlockSpec((bsz, tn), lambda j: (0, j)),
        out_shape=jax.ShapeDtypeStruct((bsz, n), jnp.float32),
        compiler_params=pltpu.CompilerParams(dimension_semantics=("arbitrary",),
                                             vmem_limit_bytes=VMEM_LIMIT),
        name="adaln",
    )(c, w, b.reshape(1, n))


def _inproj_kernel(x_ref, mod_ref, g_ref, w_ref, cos_ref, sin_ref, o_ref):
    xf = x_ref[0]
    sh = mod_ref[0, 0:1, :]
    sc = mod_ref[0, 1:2, :]
    h = (xf * _rms_scale(xf) * g_ref[...]) * (1.0 + sc) + sh
    hb = h.astype(jnp.bfloat16)
    cos = cos_ref[...]
    sin = sin_ref[...]
    lane = lax.broadcasted_iota(jnp.int32, cos.shape, 1)
    first_half = (lane % HEAD_DIM) < (HEAD_DIM // 2)
    scale = HEAD_DIM ** -0.5
    gw = DIFF_WIDTH
    for n in range(6):
        acc = lax.dot_general(hb, w_ref[:, n * gw:(n + 1) * gw], _NN,
                              preferred_element_type=jnp.float32)
        if n < 2:
            for j in range(gw // LANES):
                xb = acc[:, j * LANES:(j + 1) * LANES]
                rot = jnp.where(first_half, pltpu.roll(xb, LANES - HEAD_DIM // 2, 1),
                                pltpu.roll(xb, HEAD_DIM // 2, 1))
                yb = xb * cos + rot * sin
                if n == 0:
                    yb = yb * (scale * LOG2E)
                o_ref[0, :, n * gw + j * LANES:n * gw + (j + 1) * LANES] = yb.astype(o_ref.dtype)
        else:
            if n == 3:
                acc = acc * (0.5 * scale)
            o_ref[0, :, n * gw:(n + 1) * gw] = acc.astype(o_ref.dtype)


def _inproj_call(x, mod3, g, w_bf16, cos128, sin128):
    bsz, seq, d = x.shape
    n = w_bf16.shape[1]
    tm = TOKEN_TILE
    const = dict(pipeline_mode=pl.Buffered(1))
    return pl.pallas_call(
        _inproj_kernel,
        grid=(bsz, seq // tm),
        in_specs=[pl.BlockSpec((1, tm, d), lambda b, t: (b, t, 0)),
                  pl.BlockSpec((1, N_MOD, d), lambda b, t: (b, 0, 0)),
                  pl.BlockSpec((1, d), lambda b, t: (0, 0)),
                  pl.BlockSpec((d, n), lambda b, t: (0, 0), **const),
                  pl.BlockSpec((tm, LANES), lambda b, t: (t, 0)),
                  pl.BlockSpec((tm, LANES), lambda b, t: (t, 0))],
        out_specs=pl.BlockSpec((1, tm, n), lambda b, t: (b, t, 0)),
        out_shape=jax.ShapeDtypeStruct((bsz, seq, n), jnp.bfloat16),
        compiler_params=pltpu.CompilerParams(dimension_semantics=("arbitrary", "arbitrary"),
                                             vmem_limit_bytes=VMEM_LIMIT),
        name="inproj",
    )(x, mod3, g.reshape(1, d), w_bf16, cos128, sin128)


def _diff_kernel(lq1_ref, lk1_ref, lq2_ref, lk2_ref, g_ref, q_ref, k_ref, v_ref, o_ref,
                 vt_scr, qm_scr, acc_scr, m_scr, pre_scr, s_scr, p_scr, alpha_scr, *, lambda_init):
    g = pl.program_id(1)
    nblk = v_ref.shape[1]
    tb = ATT_BLOCK
    tw = 2 * tb
    dv = 2 * HEAD_DIM
    vrows = dv + ONES_ROWS
    nh = N_DIFF_HEADS
    last_item = (Q_GROUP - 1, nh - 1)

    @pl.when(g == 0)
    def _():
        for blk in range(nblk):
            vt = v_ref[0, blk].astype(jnp.float32).T.astype(jnp.bfloat16)
            for hd in range(nh):
                vt_scr[blk, hd * vrows:hd * vrows + dv, :] = vt[hd * dv:(hd + 1) * dv, :]
                vt_scr[blk, hd * vrows + dv:(hd + 1) * vrows, :] = jnp.ones((ONES_ROWS, tb), jnp.bfloat16)

    lane = lax.broadcasted_iota(jnp.int32, (tb, LANES), 1)
    for qb in range(Q_GROUP):
        for hd in range(nh):
            qp = q_ref[0, qb, :, hd * LANES:(hd + 1) * LANES]
            zero = jnp.zeros_like(qp)
            qm_scr[qb * nh + hd, :tb, :] = jnp.where(lane < HEAD_DIM, qp, zero)
            qm_scr[qb * nh + hd, tb:, :] = jnp.where(lane >= HEAD_DIM, qp, zero)
    acc_scr[Q_GROUP * nh - 1] = jnp.zeros((vrows, tw), jnp.float32)
    alpha_scr[...] = jnp.zeros((1, tw), jnp.float32)

    qcol = lax.broadcasted_iota(jnp.int32, (SUBLANES, tw), 1) % tb

    def scores(j, qb, hd):
        return lax.dot_general(k_ref[0, j, :, hd * LANES:(hd + 1) * LANES], qm_scr[qb * nh + hd], _NT,
                               preferred_element_type=jnp.float32)

    def elementwise(s_t, qb, hd, first):
        buf = hd % 2
        it = qb * nh + hd
        mx = None
        for v in range(tb // SUBLANES):
            sv = s_t[v * SUBLANES:(v + 1) * SUBLANES, :]
            if first:
                sv = jnp.where(qcol >= (v * SUBLANES // CHUNK) * CHUNK, sv, -jnp.inf)
            mx = sv if mx is None else jnp.maximum(mx, sv)
            s_scr[buf, v * SUBLANES:(v + 1) * SUBLANES, :] = sv
        tmax = jnp.max(mx, axis=0, keepdims=True)
        if first:
            m_new = tmax
            alpha = None
        else:
            m_old = m_scr[it]
            m_new = jnp.maximum(m_old, tmax)
            alpha = jnp.exp2(m_old - m_new)
        m_scr[it] = m_new
        return jnp.exp2(s_scr[buf] - m_new).astype(jnp.bfloat16), alpha

    def pv_product(j, hd, p):
        return lax.dot_general(vt_scr[j, hd * vrows:(hd + 1) * vrows, :], p, _NN,
                               preferred_element_type=jnp.float32)

    def finish_deferred(j_prev):
        it = Q_GROUP * nh - 1
        acc_scr[it] = alpha_scr[...] * acc_scr[it] + pv_product(j_prev, nh - 1, p_scr[...])

    def tile_body(j, blocks, nxt, j_prev):
        if j_prev is not None:
            finish_deferred(j_prev)
        items = [(qb, hd, masked) for qb, masked in blocks for hd in range(nh)]
        ss = [None] * len(items)
        ss[0] = pre_scr[...]
        issued = 1

        def issue(upto):
            nonlocal issued
            while issued <= upto:
                if issued < len(items):
                    ss[issued] = scores(j, items[issued][0], items[issued][1])
                elif issued == len(items):
                    pre_scr[...] = scores(nxt[0], nxt[1], 0)
                issued += 1

        for k, (qb, hd, masked) in enumerate(items):
            issue(k + 2)
            p, alpha = elementwise(ss[k], qb, hd, masked)
            ss[k] = None
            it = qb * nh + hd
            if (qb, hd) == last_item:
                p_scr[...] = p
                if not masked:
                    alpha_scr[...] = alpha
            elif masked:
                acc_scr[it] = pv_product(j, hd, p)
            else:
                acc_scr[it] = alpha * acc_scr[it] + pv_product(j, hd, p)

    base = g * Q_GROUP
    pre_scr[...] = scores(base + Q_GROUP - 1, Q_GROUP - 1, 0)
    j_prev = None
    for d in range(Q_GROUP - 1, -1, -1):
        blocks = [(d, True)] + [(qb, False) for qb in range(d + 1, Q_GROUP)]
        nxt = (base + d - 1, d - 1) if d > 0 else (jnp.maximum(base - 1, 0), 0)
        tile_body(base + d, blocks, nxt, j_prev)
        j_prev = base + d

    def body(t, c):
        j = base - 1 - t
        tile_body(j, [(qb, False) for qb in range(Q_GROUP)], (jnp.maximum(j - 1, 0), 0), j + 1)
        return c

    if nblk > Q_GROUP:
        lax.fori_loop(0, base, body, 0)
    finish_deferred(0)

    lam = (jnp.exp(jnp.sum(lq1_ref[...] * lk1_ref[...], axis=-1, keepdims=True))
           - jnp.exp(jnp.sum(lq2_ref[...] * lk2_ref[...], axis=-1, keepdims=True)) + lambda_init)
    for qb in range(Q_GROUP):
        for hd in range(nh):
            a1 = acc_scr[qb * nh + hd, :, :tb]
            a2 = acc_scr[qb * nh + hd, :, tb:]
            o_t = a1[:dv] / a1[dv:dv + 1] - lam * (a2[:dv] / a2[dv:dv + 1])
            ms = jnp.mean(o_t * o_t, axis=0, keepdims=True)
            o_t = o_t * lax.rsqrt(ms + NORM_EPS) * g_ref[...] * (1.0 - lambda_init)
            o_ref[0, qb * tb:(qb + 1) * tb, hd * LANES:(hd + 1) * LANES] = o_t.T.astype(o_ref.dtype)


def _diff_attn_call(proj4, lq1, lk1, lq2, lk2, subln_g, lambda_init):
    bsz, nblk, tb, _ = proj4.shape
    seq = nblk * tb
    w = DIFF_WIDTH
    vrows = 2 * HEAD_DIM + ONES_ROWS
    nit = Q_GROUP * N_DIFF_HEADS
    assert nblk % Q_GROUP == 0
    vec = lambda a: a.reshape(1, HEAD_DIM)
    small = pl.BlockSpec((1, HEAD_DIM), lambda b, g: (0, 0))
    return pl.pallas_call(
        functools.partial(_diff_kernel, lambda_init=lambda_init),
        grid=(bsz, nblk // Q_GROUP),
        in_specs=[small, small, small, small,
                  pl.BlockSpec((2 * HEAD_DIM, 1), lambda b, g: (0, 0)),
                  pl.BlockSpec((1, Q_GROUP, tb, w), lambda b, g: (b, g, 0, 0)),
                  pl.BlockSpec((1, nblk, tb, w), lambda b, g: (b, 0, 0, 1)),
                  pl.BlockSpec((1, nblk, tb, w), lambda b, g: (b, 0, 0, 2))],
        out_specs=pl.BlockSpec((1, Q_GROUP * tb, w), lambda b, g: (b, g, 0)),
        out_shape=jax.ShapeDtypeStruct((bsz, seq, w), jnp.bfloat16),
        scratch_shapes=[pltpu.VMEM((nblk, N_DIFF_HEADS * vrows, tb), jnp.bfloat16),
                        pltpu.VMEM((nit, 2 * tb, LANES), jnp.bfloat16),
                        pltpu.VMEM((nit, vrows, 2 * tb), jnp.float32),
                        pltpu.VMEM((nit, 1, 2 * tb), jnp.float32),
                        pltpu.VMEM((tb, 2 * tb), jnp.float32),
                        pltpu.VMEM((2, tb, 2 * tb), jnp.float32),
                        pltpu.VMEM((tb, 2 * tb), jnp.bfloat16),
                        pltpu.VMEM((1, 2 * tb), jnp.float32)],
        compiler_params=pltpu.CompilerParams(
            dimension_semantics=("arbitrary", "arbitrary"), vmem_limit_bytes=VMEM_LIMIT),
        name="diff_attn",
    )(vec(lq1), vec(lk1), vec(lq2), vec(lk2), subln_g.reshape(2 * HEAD_DIM, 1), proj4, proj4, proj4)


def _sb_kernel(q_ref, k_ref, v_ref, o_ref, kp_scr, vt_scr, qm_scr, acc_scr, pre_scr, ap_scr, carry_scr, a_scr):
    g = pl.program_id(1)
    nblk = k_ref.shape[1]
    tb = ATT_BLOCK
    nh = N_SB_HEADS
    last_item = (Q_GROUP - 1, nh - 1)

    @pl.when(g == 0)
    def _():
        row = lax.broadcasted_iota(jnp.int32, (tb, tb), 0)
        col = lax.broadcasted_iota(jnp.int32, (tb, tb), 1)
        pos = SEG_LEN * (row % SUBLANES) + (SEG_LEN - 1 - row // SUBLANES)
        perm = jnp.where(col == pos, 1.0, 0.0).astype(jnp.bfloat16)
        for blk in range(nblk):
            kp = lax.dot_general(perm, k_ref[0, blk], _NN, preferred_element_type=jnp.float32)
            kp_scr[blk] = kp.astype(jnp.bfloat16)
            vp = lax.dot_general(perm, v_ref[0, blk], _NN, preferred_element_type=jnp.float32)
            vt_scr[blk] = vp.T.astype(jnp.bfloat16)

    lane = lax.broadcasted_iota(jnp.int32, (tb, LANES), 1)
    for qb in range(Q_GROUP):
        for pr in range(N_PAIRS):
            qp = q_ref[0, qb, :, pr * LANES:(pr + 1) * LANES]
            zero = jnp.zeros_like(qp)
            qm_scr[qb * N_PAIRS + pr, :tb, :] = jnp.where(lane < HEAD_DIM, qp, zero)
            qm_scr[qb * N_PAIRS + pr, tb:, :] = jnp.where(lane >= HEAD_DIM, qp, zero)
        for h in range(nh):
            carry_scr[qb * nh + h] = jnp.ones((SUBLANES, tb), jnp.float32)
    acc_scr[last_item[0], last_item[1] * HEAD_DIM:, :] = jnp.zeros((HEAD_DIM, tb), jnp.float32)

    sub = lax.broadcasted_iota(jnp.int32, (SUBLANES, tb), 0)
    qcol = lax.broadcasted_iota(jnp.int32, (SUBLANES, tb), 1)

    def scores(j, qb, pair):
        return lax.dot_general(kp_scr[j, :, pair * LANES:(pair + 1) * LANES], qm_scr[qb * N_PAIRS + pair], _NT,
                               preferred_element_type=jnp.float32)

    def elementwise(zh, qb, h, diag):
        buf = h % 2
        rem = jnp.ones((SUBLANES, tb), jnp.float32)
        prev = None
        for v in range(SEG_LEN):
            beta = 0.5 * jnp.tanh(zh[v * SUBLANES:(v + 1) * SUBLANES, :]) + 0.5
            if diag:
                beta = jnp.where(SEG_LEN * sub + (SEG_LEN - 1 - v) < qcol, beta, 0.0)
            a_v = beta * rem
            rem = rem - a_v
            if v % 2 == 0:
                prev = a_v
            else:
                ap_scr[buf, (v - 1) * SUBLANES:(v + 1) * SUBLANES, :] = jnp.concatenate(
                    [prev, a_v], axis=0).astype(jnp.bfloat16)
        incl = rem
        for k in (1, 2, 4):
            shifted = pltpu.roll(incl, SUBLANES - k, 0)
            incl = incl * jnp.where(sub + k < SUBLANES, shifted, 1.0)
        excl = jnp.where(sub + 1 < SUBLANES, pltpu.roll(incl, SUBLANES - 1, 0), 1.0)
        carry = carry_scr[qb * nh + h]
        start = carry * excl
        carry_scr[qb * nh + h] = carry * jnp.broadcast_to(incl[0:1, :], carry.shape)
        start2 = jnp.concatenate([start, start], axis=0).astype(jnp.bfloat16)
        return jnp.concatenate(
            [ap_scr[buf, u * 2 * SUBLANES:(u + 1) * 2 * SUBLANES, :] * start2 for u in range(SEG_LEN // 2)],
            axis=0)

    def av_product(j, h, a):
        return lax.dot_general(vt_scr[j, h * HEAD_DIM:(h + 1) * HEAD_DIM, :], a, _NN,
                               preferred_element_type=jnp.float32)

    def finish_deferred(j_prev):
        qb, h = last_item
        acc_scr[qb, h * HEAD_DIM:, :] += av_product(j_prev, h, a_scr[...])

    def tile_body(j, blocks, nxt, j_prev):
        if j_prev is not None:
            finish_deferred(j_prev)
        pairs = [(qb, pr, masked) for qb, masked in blocks for pr in range(N_PAIRS)]
        zs = [None] * len(pairs)
        zs[0] = pre_scr[...]
        issued = 1

        def issue(upto):
            nonlocal issued
            while issued <= upto:
                if issued < len(pairs):
                    zs[issued] = scores(j, pairs[issued][0], pairs[issued][1])
                elif issued == len(pairs):
                    pre_scr[...] = scores(nxt[0], nxt[1], 0)
                issued += 1

        for k, (qb, pr, masked) in enumerate(pairs):
            issue(k + 2)
            for hh in range(2):
                h = 2 * pr + hh
                a = elementwise(zs[k][:, hh * tb:(hh + 1) * tb], qb, h, masked)
                rows = slice(h * HEAD_DIM, (h + 1) * HEAD_DIM)
                first_write = masked
                if (qb, h) == last_item:
                    a_scr[...] = a
                elif first_write:
                    acc_scr[qb, rows, :] = av_product(j, h, a)
                else:
                    acc_scr[qb, rows, :] += av_product(j, h, a)
            zs[k] = None

    base = g * Q_GROUP
    pre_scr[...] = scores(base + Q_GROUP - 1, Q_GROUP - 1, 0)
    j_prev = None
    for d in range(Q_GROUP - 1, -1, -1):
        blocks = [(d, True)] + [(qb, False) for qb in range(d + 1, Q_GROUP)]
        nxt = (base + d - 1, d - 1) if d > 0 else (jnp.maximum(base - 1, 0), 0)
        tile_body(base + d, blocks, nxt, j_prev)
        j_prev = base + d

    def body(t, c):
        j = base - 1 - t
        tile_body(j, [(qb, False) for qb in range(Q_GROUP)], (jnp.maximum(j - 1, 0), 0), j + 1)
        return c

    if nblk > Q_GROUP:
        lax.fori_loop(0, base, body, 0)
    finish_deferred(0)
    for qb in range(Q_GROUP):
        o_ref[0, qb * tb:(qb + 1) * tb, :] = acc_scr[qb].T.astype(o_ref.dtype)


def _sb_attn_call(proj4):
    bsz, nblk, tb, _ = proj4.shape
    seq = nblk * tb
    w = SB_WIDTH
    base = 3 * DIFF_WIDTH // w
    assert nblk % Q_GROUP == 0
    return pl.pallas_call(
        _sb_kernel,
        grid=(bsz, nblk // Q_GROUP),
        in_specs=[pl.BlockSpec((1, Q_GROUP, tb, w), lambda b, g: (b, g, 0, base)),
                  pl.BlockSpec((1, nblk, tb, w), lambda b, g: (b, 0, 0, base + 1)),
                  pl.BlockSpec((1, nblk, tb, w), lambda b, g: (b, 0, 0, base + 2))],
        out_specs=pl.BlockSpec((1, Q_GROUP * tb, w), lambda b, g: (b, g, 0)),
        out_shape=jax.ShapeDtypeStruct((bsz, seq, w), jnp.bfloat16),
        scratch_shapes=[pltpu.VMEM((nblk, tb, w), jnp.bfloat16),
                        pltpu.VMEM((nblk, w, tb), jnp.bfloat16),
                        pltpu.VMEM((Q_GROUP * N_PAIRS, 2 * tb, LANES), jnp.bfloat16),
                        pltpu.VMEM((Q_GROUP, w, tb), jnp.float32),
                        pltpu.VMEM((tb, 2 * tb), jnp.float32),
                        pltpu.VMEM((2, tb, tb), jnp.bfloat16),
                        pltpu.VMEM((Q_GROUP * N_SB_HEADS, SUBLANES, tb), jnp.float32),
                        pltpu.VMEM((tb, tb), jnp.bfloat16)],
        compiler_params=pltpu.CompilerParams(
            dimension_semantics=("arbitrary", "arbitrary"), vmem_limit_bytes=VMEM_LIMIT),
        name="sb_attn",
    )(proj4, proj4, proj4)


def _out_ffn_kernel(x_ref, od_ref, os_ref, mod_ref, wo_ref, gf_ref, w1_ref, w2_ref, gl_ref, o_ref):
    g_m = mod_ref[0, 2:3, :]
    sh_f = mod_ref[0, 3:4, :]
    sc_f = mod_ref[0, 4:5, :]
    g_f = mod_ref[0, 5:6, :]
    mixed = (lax.dot_general(od_ref[0], wo_ref[:DIFF_WIDTH, :], _NN, preferred_element_type=jnp.float32)
             + lax.dot_general(os_ref[0], wo_ref[DIFF_WIDTH:, :], _NN, preferred_element_type=jnp.float32))
    x1 = x_ref[0] + g_m * mixed
    h = ((x1 * _rms_scale(x1) * gf_ref[...]) * (1.0 + sc_f) + sh_f).astype(jnp.bfloat16)
    d_ff = w1_ref.shape[1]
    fc = 1024
    f = None
    for c in range(d_ff // fc):
        u = lax.dot_general(h, w1_ref[:, c * fc:(c + 1) * fc], _NN, preferred_element_type=jnp.float32)
        r = jnp.square(jnp.maximum(u, 0.0)).astype(jnp.bfloat16)
        part = lax.dot_general(r, w2_ref[c * fc:(c + 1) * fc, :], _NN, preferred_element_type=jnp.float32)
        f = part if f is None else f + part
    x2 = x1 + g_f * f
    o_ref[0] = x2 * _rms_scale(x2) * gl_ref[...]


def _out_ffn_call(x, o_diff, o_sb, mod3, wo, gf, w1, w2, gl):
    bsz, seq, d = x.shape
    tm = TOKEN_TILE
    const = dict(pipeline_mode=pl.Buffered(1))
    tok = lambda w: pl.BlockSpec((1, tm, w), lambda b, t: (b, t, 0))
    return pl.pallas_call(
        _out_ffn_kernel,
        grid=(bsz, seq // tm),
        in_specs=[tok(d), tok(DIFF_WIDTH), tok(SB_WIDTH),
                  pl.BlockSpec((1, N_MOD, d), lambda b, t: (b, 0, 0)),
                  pl.BlockSpec(wo.shape, lambda b, t: (0, 0), **const),
                  pl.BlockSpec((1, d), lambda b, t: (0, 0)),
                  pl.BlockSpec(w1.shape, lambda b, t: (0, 0), **const),
                  pl.BlockSpec(w2.shape, lambda b, t: (0, 0), **const),
                  pl.BlockSpec((1, d), lambda b, t: (0, 0))],
        out_specs=tok(d),
        out_shape=jax.ShapeDtypeStruct((bsz, seq, d), jnp.float32),
        compiler_params=pltpu.CompilerParams(dimension_semantics=("arbitrary", "arbitrary"),
                                             vmem_limit_bytes=VMEM_LIMIT),
        name="out_ffn",
    )(x, o_diff, o_sb, mod3, wo, gf.reshape(1, d), w1, w2, gl.reshape(1, d))


def _rope_tables(seq_len):
    dim = HEAD_DIM
    inv = 1.0 / (ROPE_THETA ** (jnp.arange(0, dim, 2, dtype=jnp.float32) / dim))
    ang = jnp.arange(seq_len, dtype=jnp.float32)[:, None] * inv[None, :]
    ang = jnp.concatenate([ang, ang], axis=-1)
    cos, sin = jnp.cos(ang), jnp.sin(ang)
    sign = jnp.where(jnp.arange(dim) < dim // 2, -1.0, 1.0).astype(jnp.float32)
    reps = LANES // dim
    return jnp.tile(cos, (1, reps)), jnp.tile(sin * sign[None, :], (1, reps))


def kernel(x, c, ada_w, ada_b, mix_norm_g, w_in, lambda_q1, lambda_k1, lambda_q2, lambda_k2,
           diff_subln_g, w_out, ffn_norm_g, w_ff1, w_ff2, final_norm_g):
    bsz, seq, d = x.shape
    depth = ada_w.shape[0]
    assert depth == 1, "the fused final norm assumes a single layer"
    assert seq % TOKEN_TILE == 0 and seq % ATT_BLOCK == 0
    cos128, sin128 = _rope_tables(seq)
    bf = jnp.bfloat16
    layer = 0
    lambda_init = 0.8 - 0.6 * math.exp(-0.3 * layer)
    mod3 = _adaln_call(c, ada_w[layer], ada_b[layer]).reshape(bsz, N_MOD, d)
    proj = _inproj_call(x, mod3, mix_norm_g[layer], w_in[layer].astype(bf), cos128, sin128)
    proj4 = proj.reshape(bsz, seq // ATT_BLOCK, ATT_BLOCK, proj.shape[-1])
    o_diff = _diff_attn_call(proj4, lambda_q1[layer], lambda_k1[layer], lambda_q2[layer],
                             lambda_k2[layer], diff_subln_g[layer], lambda_init)
    o_sb = _sb_attn_call(proj4)
    return _out_ffn_call(x, o_diff, o_sb, mod3, w_out[layer].astype(bf), ffn_norm_g[layer],
                         w_ff1[layer].astype(bf), w_ff2[layer].astype(bf), final_norm_g)
```

```python
import functools
import math

import jax
import jax.numpy as jnp
from jax import lax
from jax.experimental import pallas as pl
from jax.experimental.pallas import tpu as pltpu

HEAD_DIM = 64
N_DIFF_HEADS = 4
N_SB_HEADS = 8
DIFF_WIDTH = N_DIFF_HEADS * 2 * HEAD_DIM
SB_WIDTH = N_SB_HEADS * HEAD_DIM
CHUNK = 64
ROPE_THETA = 10000.0
NORM_EPS = 1e-6
N_MOD = 6

LANES = 128
SUBLANES = 8
ATT_BLOCK = 256
SEG_LEN = ATT_BLOCK // SUBLANES
TOKEN_TILE = 512
N_PAIRS = N_SB_HEADS // 2
Q_GROUP = 4
ONES_ROWS = 16
LOG2E = 1.4426950408889634
VMEM_LIMIT = 56 * 1024 * 1024

_NT = (((1,), (1,)), ((), ()))
_NN = (((1,), (0,)), ((), ()))


def _rms_scale(xf):
    return lax.rsqrt(jnp.mean(xf * xf, axis=-1, keepdims=True) + NORM_EPS)


def _adaln_kernel(c_ref, w_ref, b_ref, o_ref):
    c = c_ref[...]
    ca = c / (1.0 + jnp.exp(-c))
    o_ref[...] = lax.dot_general(ca.astype(jnp.bfloat16), w_ref[...].astype(jnp.bfloat16), _NN,
                                 preferred_element_type=jnp.float32) + b_ref[...]


def _adaln_call(c, w, b):
    bsz, d = c.shape
    n = w.shape[1]
    tn = 1024
    return pl.pallas_call(
        _adaln_kernel,
        grid=(n // tn,),
        in_specs=[pl.BlockSpec((bsz, d), lambda j: (0, 0)),
                  pl.BlockSpec((d, tn), lambda j: (0, j)),
                  pl.BlockSpec((1, tn), lambda j: (0, j))],
        out_specs=pl.BlockSpec((bsz, tn), lambda j: (0, j)),
        out_shape=jax.ShapeDtypeStruct((bsz, n), jnp.float32),
        compiler_params=pltpu.CompilerParams(dimension_semantics=("arbitrary",),
                                             vmem_limit_bytes=VMEM_LIMIT),
        name="adaln",
    )(c, w, b.reshape(1, n))


def _inproj_kernel(x_ref, mod_ref, g_ref, w_ref, cos_ref, sin_ref, o_ref):
    xf = x_ref[0]
    sh = mod_ref[0, 0:1, :]
    sc = mod_ref[0, 1:2, :]
    h = (xf * _rms_scale(xf) * g_ref[...]) * (1.0 + sc) + sh
    hb = h.astype(jnp.bfloat16)
    cos = cos_ref[...]
    sin = sin_ref[...]
    lane = lax.broadcasted_iota(jnp.int32, cos.shape, 1)
    first_half = (lane % HEAD_DIM) < (HEAD_DIM // 2)
    scale = HEAD_DIM ** -0.5
    gw = DIFF_WIDTH
    for n in range(6):
        acc = lax.dot_general(hb, w_ref[:, n * gw:(n + 1) * gw], _NN,
                              preferred_element_type=jnp.float32)
        if n < 2:
            for j in range(gw // LANES):
                xb = acc[:, j * LANES:(j + 1) * LANES]
                rot = jnp.where(first_half, pltpu.roll(xb, LANES - HEAD_DIM // 2, 1),
                                pltpu.roll(xb, HEAD_DIM // 2, 1))
                yb = xb * cos + rot * sin
                if n == 0:
                    yb = yb * (scale * LOG2E)
                o_ref[0, :, n * gw + j * LANES:n * gw + (j + 1) * LANES] = yb.astype(o_ref.dtype)
        else:
            if n == 3:
                acc = acc * (0.5 * scale)
            o_ref[0, :, n * gw:(n + 1) * gw] = acc.astype(o_ref.dtype)


def _inproj_call(x, mod3, g, w_bf16, cos128, sin128):
    bsz, seq, d = x.shape
    n = w_bf16.shape[1]
    tm = TOKEN_TILE
    const = dict(pipeline_mode=pl.Buffered(1))
    return pl.pallas_call(
        _inproj_kernel,
        grid=(bsz, seq // tm),
        in_specs=[pl.BlockSpec((1, tm, d), lambda b, t: (b, t, 0)),
                  pl.BlockSpec((1, N_MOD, d), lambda b, t: (b, 0, 0)),
                  pl.BlockSpec((1, d), lambda b, t: (0, 0)),
                  pl.BlockSpec((d, n), lambda b, t: (0, 0), **const),
                  pl.BlockSpec((tm, LANES), lambda b, t: (t, 0)),
                  pl.BlockSpec((tm, LANES), lambda b, t: (t, 0))],
        out_specs=pl.BlockSpec((1, tm, n), lambda b, t: (b, t, 0)),
        out_shape=jax.ShapeDtypeStruct((bsz, seq, n), jnp.bfloat16),
        compiler_params=pltpu.CompilerParams(dimension_semantics=("arbitrary", "arbitrary"),
                                             vmem_limit_bytes=VMEM_LIMIT),
        name="inproj",
    )(x, mod3, g.reshape(1, d), w_bf16, cos128, sin128)


def _attn_kernel(lq1_ref, lk1_ref, lq2_ref, lk2_ref, g_ref, dq_ref, dk_ref, dv_ref, sq_ref, sk_ref, sv_ref,
                 od_ref, os_ref,
                 dvt_scr, dqm_scr, dacc_scr, m_scr, s_scr, p_scr, alpha_scr,
                 kp_scr, svt_scr, sqm_scr, sacc_scr, pre_scr, ap_scr, carry_scr, *, lambda_init):
    g = pl.program_id(1)
    nblk = dk_ref.shape[1]
    tb = ATT_BLOCK
    tw = 2 * tb
    dv = 2 * HEAD_DIM
    vrows = dv + ONES_ROWS
    nd = N_DIFF_HEADS
    ns = N_SB_HEADS
    d_last = Q_GROUP * nd - 1

    @pl.when(g == 0)
    def _():
        for blk in range(nblk):
            vt = dv_ref[0, blk].astype(jnp.float32).T.astype(jnp.bfloat16)
            for hd in range(nd):
                dvt_scr[blk, hd * vrows:hd * vrows + dv, :] = vt[hd * dv:(hd + 1) * dv, :]
                dvt_scr[blk, hd * vrows + dv:(hd + 1) * vrows, :] = jnp.ones((ONES_ROWS, tb), jnp.bfloat16)
        row = lax.broadcasted_iota(jnp.int32, (tb, tb), 0)
        col = lax.broadcasted_iota(jnp.int32, (tb, tb), 1)
        pos = SEG_LEN * (row % SUBLANES) + (SEG_LEN - 1 - row // SUBLANES)
        perm = jnp.where(col == pos, 1.0, 0.0).astype(jnp.bfloat16)
        for blk in range(nblk):
            kp = lax.dot_general(perm, sk_ref[0, blk], _NN, preferred_element_type=jnp.float32)
            kp_scr[blk] = kp.astype(jnp.bfloat16)
            vp = lax.dot_general(perm, sv_ref[0, blk], _NN, preferred_element_type=jnp.float32)
            svt_scr[blk] = vp.T.astype(jnp.bfloat16)

    drow = lax.broadcasted_iota(jnp.int32, (LANES, tb), 0)
    for qb in range(Q_GROUP):
        for c in range(nd):
            for ref, scr in ((dq_ref, dqm_scr), (sq_ref, sqm_scr)):
                qt = ref[0, qb, :, c * LANES:(c + 1) * LANES].astype(jnp.float32).T.astype(jnp.bfloat16)
                zero = jnp.zeros_like(qt)
                scr[qb * nd + c, :, :tb] = jnp.where(drow < HEAD_DIM, qt, zero)
                scr[qb * nd + c, :, tb:] = jnp.where(drow >= HEAD_DIM, qt, zero)
        for h in range(ns):
            carry_scr[qb * ns + h] = jnp.ones((SUBLANES, tb), jnp.float32)
    dacc_scr[d_last] = jnp.zeros((vrows, tw), jnp.float32)
    alpha_scr[...] = jnp.zeros((1, tw), jnp.float32)

    sub = lax.broadcasted_iota(jnp.int32, (SUBLANES, tb), 0)
    qcol = lax.broadcasted_iota(jnp.int32, (SUBLANES, tb), 1)
    qcol2 = lax.broadcasted_iota(jnp.int32, (SUBLANES, tw), 1) % tb

    def d_scores(j, qb, hd):
        return lax.dot_general(dk_ref[0, j, :, hd * LANES:(hd + 1) * LANES], dqm_scr[qb * nd + hd], _NN,
                               preferred_element_type=jnp.float32)

    def d_elementwise(s_t, qb, hd, first):
        buf = hd % 2
        it = qb * nd + hd
        mx = None
        for v in range(tb // SUBLANES):
            sv = s_t[v * SUBLANES:(v + 1) * SUBLANES, :]
            if first:
                sv = jnp.where(qcol2 >= (v * SUBLANES // CHUNK) * CHUNK, sv, -jnp.inf)
            mx = sv if mx is None else jnp.maximum(mx, sv)
            s_scr[buf, v * SUBLANES:(v + 1) * SUBLANES, :] = sv
        tmax = jnp.max(mx, axis=0, keepdims=True)
        if first:
            m_new = tmax
            alpha = None
        else:
            m_old = m_scr[it]
            m_new = jnp.maximum(m_old, tmax)
            alpha = jnp.exp2(m_old - m_new)
        m_scr[it] = m_new
        return jnp.exp2(s_scr[buf] - m_new).astype(jnp.bfloat16), alpha

    def d_pv(j, hd, p):
        return lax.dot_general(dvt_scr[j, hd * vrows:(hd + 1) * vrows, :], p, _NN,
                               preferred_element_type=jnp.float32)

    def finish_deferred(j_prev):
        dacc_scr[d_last] = alpha_scr[...] * dacc_scr[d_last] + d_pv(j_prev, nd - 1, p_scr[...])

    def s_scores(j, qb, pair):
        return lax.dot_general(kp_scr[j, :, pair * LANES:(pair + 1) * LANES], sqm_scr[qb * N_PAIRS + pair], _NN,
                               preferred_element_type=jnp.float32)

    def s_elementwise(zh, qb, h, diag):
        buf = h % 2
        rem = jnp.ones((SUBLANES, tb), jnp.float32)
        prev = None
        for v in range(SEG_LEN):
            beta = 0.5 * jnp.tanh(zh[v * SUBLANES:(v + 1) * SUBLANES, :]) + 0.5
            if diag:
                beta = jnp.where(SEG_LEN * sub + (SEG_LEN - 1 - v) < qcol, beta, 0.0)
            a_v = beta * rem
            rem = rem - a_v
            if v % 2 == 0:
                prev = a_v
            else:
                ap_scr[buf, (v - 1) * SUBLANES:(v + 1) * SUBLANES, :] = jnp.concatenate(
                    [prev, a_v], axis=0).astype(jnp.bfloat16)
        incl = rem
        for k in (1, 2, 4):
            shifted = pltpu.roll(incl, SUBLANES - k, 0)
            incl = incl * jnp.where(sub + k < SUBLANES, shifted, 1.0)
        excl = jnp.where(sub + 1 < SUBLANES, pltpu.roll(incl, SUBLANES - 1, 0), 1.0)
        carry = carry_scr[qb * ns + h]
        start = carry * excl
        carry_scr[qb * ns + h] = carry * jnp.broadcast_to(incl[0:1, :], carry.shape)
        start2 = jnp.concatenate([start, start], axis=0).astype(jnp.bfloat16)
        return jnp.concatenate(
            [ap_scr[buf, u * 2 * SUBLANES:(u + 1) * 2 * SUBLANES, :] * start2 for u in range(SEG_LEN // 2)],
            axis=0)

    def s_av(j, h, a):
        return lax.dot_general(svt_scr[j, h * HEAD_DIM:(h + 1) * HEAD_DIM, :], a, _NN,
                               preferred_element_type=jnp.float32)

    def tile_body(j, blocks, nxt, j_prev):
        if j_prev is not None:
            finish_deferred(j_prev)
        entries = []
        for qb, masked in blocks:
            for c in range(nd):
                entries.append(("s", qb, c, masked))
                entries.append(("d", qb, c, masked))
        sc = [None] * len(entries)
        sc[0] = pre_scr[...]
        issued = 1

        def issue(upto):
            nonlocal issued
            while issued <= upto:
                if issued < len(entries):
                    kind, qb, c, _ = entries[issued]
                    sc[issued] = s_scores(j, qb, c) if kind == "s" else d_scores(j, qb, c)
                elif issued == len(entries):
                    pre_scr[...] = s_scores(nxt[0], nxt[1], 0)
                issued += 1

        for k, (kind, qb, c, masked) in enumerate(entries):
            issue(k + 2)
            if kind == "s":
                for hh in range(2):
                    h = 2 * c + hh
                    a = s_elementwise(sc[k][:, hh * tb:(hh + 1) * tb], qb, h, masked)
                    rows = slice(h * HEAD_DIM, (h + 1) * HEAD_DIM)
                    if masked:
                        sacc_scr[qb, rows, :] = s_av(j, h, a)
                    else:
                        sacc_scr[qb, rows, :] += s_av(j, h, a)
            else:
                p, alpha = d_elementwise(sc[k], qb, c, masked)
                it = qb * nd + c
                if it == d_last:
                    p_scr[...] = p
                    if not masked:
                        alpha_scr[...] = alpha
                elif masked:
                    dacc_scr[it] = d_pv(j, c, p)
                else:
                    dacc_scr[it] = alpha * dacc_scr[it] + d_pv(j, c, p)
            sc[k] = None

    base = g * Q_GROUP
    pre_scr[...] = s_scores(base + Q_GROUP - 1, Q_GROUP - 1, 0)
    j_prev = None
    for d in range(Q_GROUP - 1, -1, -1):
        blocks = [(d, True)] + [(qb, False) for qb in range(d + 1, Q_GROUP)]
        nxt = (base + d - 1, d - 1) if d > 0 else (jnp.maximum(base - 1, 0), 0)
        tile_body(base + d, blocks, nxt, j_prev)
        j_prev = base + d

    def body(t, c):
        j = base - 1 - t
        tile_body(j, [(qb, False) for qb in range(Q_GROUP)], (jnp.maximum(j - 1, 0), 0), j + 1)
        return c

    if nblk > Q_GROUP:
        lax.fori_loop(0, base, body, 0)
    finish_deferred(0)

    lam = (jnp.exp(jnp.sum(lq1_ref[...] * lk1_ref[...], axis=-1, keepdims=True))
           - jnp.exp(jnp.sum(lq2_ref[...] * lk2_ref[...], axis=-1, keepdims=True)) + lambda_init)
    for qb in range(Q_GROUP):
        for hd in range(nd):
            a1 = dacc_scr[qb * nd + hd, :, :tb]
            a2 = dacc_scr[qb * nd + hd, :, tb:]
            o_t = a1[:dv] / a1[dv:dv + 1] - lam * (a2[:dv] / a2[dv:dv + 1])
            ms = jnp.mean(o_t * o_t, axis=0, keepdims=True)
            o_t = o_t * lax.rsqrt(ms + NORM_EPS) * g_ref[...] * (1.0 - lambda_init)
            od_ref[0, qb * tb:(qb + 1) * tb, hd * LANES:(hd + 1) * LANES] = o_t.T.astype(od_ref.dtype)
        os_ref[0, qb * tb:(qb + 1) * tb, :] = sacc_scr[qb].T.astype(os_ref.dtype)


def _attn_call(proj4, lq1, lk1, lq2, lk2, subln_g, lambda_init):
    bsz, nblk, tb, _ = proj4.shape
    seq = nblk * tb
    w = DIFF_WIDTH
    assert SB_WIDTH == w and nblk % Q_GROUP == 0
    vrows = 2 * HEAD_DIM + ONES_ROWS
    nit = Q_GROUP * N_DIFF_HEADS
    vec = lambda a: a.reshape(1, HEAD_DIM)
    small = pl.BlockSpec((1, HEAD_DIM), lambda b, g: (0, 0))
    qspec = lambda col: pl.BlockSpec((1, Q_GROUP, tb, w), lambda b, g: (b, g, 0, col))
    kvspec = lambda col: pl.BlockSpec((1, nblk, tb, w), lambda b, g: (b, 0, 0, col))
    ospec = pl.BlockSpec((1, Q_GROUP * tb, w), lambda b, g: (b, g, 0))
    oshape = jax.ShapeDtypeStruct((bsz, seq, w), jnp.bfloat16)
    return pl.pallas_call(
        functools.partial(_attn_kernel, lambda_init=lambda_init),
        grid=(bsz, nblk // Q_GROUP),
        in_specs=[small, small, small, small,
                  pl.BlockSpec((2 * HEAD_DIM, 1), lambda b, g: (0, 0)),
                  qspec(0), kvspec(1), kvspec(2), qspec(3), kvspec(4), kvspec(5)],
        out_specs=[ospec, ospec],
        out_shape=[oshape, oshape],
        scratch_shapes=[pltpu.VMEM((nblk, N_DIFF_HEADS * vrows, tb), jnp.bfloat16),
                        pltpu.VMEM((nit, LANES, 2 * tb), jnp.bfloat16),
                        pltpu.VMEM((nit, vrows, 2 * tb), jnp.float32),
                        pltpu.VMEM((nit, 1, 2 * tb), jnp.float32),
                        pltpu.VMEM((2, tb, 2 * tb), jnp.float32),
                        pltpu.VMEM((tb, 2 * tb), jnp.bfloat16),
                        pltpu.VMEM((1, 2 * tb), jnp.float32),
                        pltpu.VMEM((nblk, tb, w), jnp.bfloat16),
                        pltpu.VMEM((nblk, w, tb), jnp.bfloat16),
                        pltpu.VMEM((Q_GROUP * N_PAIRS, LANES, 2 * tb), jnp.bfloat16),
                        pltpu.VMEM((Q_GROUP, w, tb), jnp.float32),
                        pltpu.VMEM((tb, 2 * tb), jnp.float32),
                        pltpu.VMEM((2, tb, tb), jnp.bfloat16),
                        pltpu.VMEM((Q_GROUP * N_SB_HEADS, SUBLANES, tb), jnp.float32)],
        compiler_params=pltpu.CompilerParams(
            dimension_semantics=("arbitrary", "arbitrary"), vmem_limit_bytes=VMEM_LIMIT),
        name="attn",
    )(vec(lq1), vec(lk1), vec(lq2), vec(lk2), subln_g.reshape(2 * HEAD_DIM, 1),
      proj4, proj4, proj4, proj4, proj4, proj4)


def _out_ffn_kernel(x_ref, od_ref, os_ref, mod_ref, wo_ref, gf_ref, w1_ref, w2_ref, gl_ref, o_ref):
    g_m = mod_ref[0, 2:3, :]
    sh_f = mod_ref[0, 3:4, :]
    sc_f = mod_ref[0, 4:5, :]
    g_f = mod_ref[0, 5:6, :]
    mixed = (lax.dot_general(od_ref[0], wo_ref[:DIFF_WIDTH, :], _NN, preferred_element_type=jnp.float32)
             + lax.dot_general(os_ref[0], wo_ref[DIFF_WIDTH:, :], _NN, preferred_element_type=jnp.float32))
    x1 = x_ref[0] + g_m * mixed
    h = ((x1 * _rms_scale(x1) * gf_ref[...]) * (1.0 + sc_f) + sh_f).astype(jnp.bfloat16)
    d_ff = w1_ref.shape[1]
    fc = 1024
    f = None
    for c in range(d_ff // fc):
        u = lax.dot_general(h, w1_ref[:, c * fc:(c + 1) * fc], _NN, preferred_element_type=jnp.float32)
        r = jnp.square(jnp.maximum(u, 0.0)).astype(jnp.bfloat16)
        part = lax.dot_general(r, w2_ref[c * fc:(c + 1) * fc, :], _NN, preferred_element_type=jnp.float32)
        f = part if f is None else f + part
    x2 = x1 + g_f * f
    o_ref[0] = x2 * _rms_scale(x2) * gl_ref[...]


def _out_ffn_call(x, o_diff, o_sb, mod3, wo, gf, w1, w2, gl):
    bsz, seq, d = x.shape
    tm = TOKEN_TILE
    const = dict(pipeline_mode=pl.Buffered(1))
    tok = lambda w: pl.BlockSpec((1, tm, w), lambda b, t: (b, t, 0))
    return pl.pallas_call(
        _out_ffn_kernel,
        grid=(bsz, seq // tm),
        in_specs=[tok(d), tok(DIFF_WIDTH), tok(SB_WIDTH),
                  pl.BlockSpec((1, N_MOD, d), lambda b, t: (b, 0, 0)),
                  pl.BlockSpec(wo.shape, lambda b, t: (0, 0), **const),
                  pl.BlockSpec((1, d), lambda b, t: (0, 0)),
                  pl.BlockSpec(w1.shape, lambda b, t: (0, 0), **const),
                  pl.BlockSpec(w2.shape, lambda b, t: (0, 0), **const),
                  pl.BlockSpec((1, d), lambda b, t: (0, 0))],
        out_specs=tok(d),
        out_shape=jax.ShapeDtypeStruct((bsz, seq, d), jnp.float32),
        compiler_params=pltpu.CompilerParams(dimension_semantics=("arbitrary", "arbitrary"),
                                             vmem_limit_bytes=VMEM_LIMIT),
        name="out_ffn",
    )(x, o_diff, o_sb, mod3, wo, gf.reshape(1, d), w1, w2, gl.reshape(1, d))


def _rope_tables(seq_len):
    dim = HEAD_DIM
    inv = 1.0 / (ROPE_THETA ** (jnp.arange(0, dim, 2, dtype=jnp.float32) / dim))
    ang = jnp.arange(seq_len, dtype=jnp.float32)[:, None] * inv[None, :]
    ang = jnp.concatenate([ang, ang], axis=-1)
    cos, sin = jnp.cos(ang), jnp.sin(ang)
    sign = jnp.where(jnp.arange(dim) < dim // 2, -1.0, 1.0).astype(jnp.float32)
    reps = LANES // dim
    return jnp.tile(cos, (1, reps)), jnp.tile(sin * sign[None, :], (1, reps))


def kernel(x, c, ada_w, ada_b, mix_norm_g, w_in, lambda_q1, lambda_k1, lambda_q2, lambda_k2,
           diff_subln_g, w_out, ffn_norm_g, w_ff1, w_ff2, final_norm_g):
    bsz, seq, d = x.shape
    depth = ada_w.shape[0]
    assert depth == 1, "the fused final norm assumes a single layer"
    assert seq % TOKEN_TILE == 0 and seq % ATT_BLOCK == 0
    cos128, sin128 = _rope_tables(seq)
    bf = jnp.bfloat16
    layer = 0
    lambda_init = 0.8 - 0.6 * math.exp(-0.3 * layer)
    mod3 = _adaln_call(c, ada_w[layer], ada_b[layer]).reshape(bsz, N_MOD, d)
    proj = _inproj_call(x, mod3, mix_norm_g[layer], w_in[layer].astype(bf), cos128, sin128)
    proj4 = proj.reshape(bsz, seq // ATT_BLOCK, ATT_BLOCK, proj.shape[-1])
    o_diff, o_sb = _attn_call(proj4, lambda_q1[layer], lambda_k1[layer], lambda_q2[layer],
                              lambda_k2[layer], diff_subln_g[layer], lambda_init)
    return _out_ffn_call(x, o_diff, o_sb, mod3, w_out[layer].astype(bf), ffn_norm_g[layer],
                         w_ff1[layer].astype(bf), w_ff2[layer].astype(bf), final_norm_g)
```

```python
import functools
import math

import jax
import jax.numpy as jnp
from jax import lax
from jax.experimental import pallas as pl
from jax.experimental.pallas import tpu as pltpu

HEAD_DIM = 64
N_DIFF_HEADS = 4
N_SB_HEADS = 8
DIFF_WIDTH = N_DIFF_HEADS * 2 * HEAD_DIM
SB_WIDTH = N_SB_HEADS * HEAD_DIM
CHUNK = 64
ROPE_THETA = 10000.0
NORM_EPS = 1e-6
N_MOD = 6

LANES = 128
SUBLANES = 8
ATT_BLOCK = 256
SEG_LEN = ATT_BLOCK // SUBLANES
TOKEN_TILE = 512
N_PAIRS = N_SB_HEADS // 2
Q_GROUP = 4
ONES_ROWS = 16
LOG2E = 1.4426950408889634
VMEM_LIMIT = 56 * 1024 * 1024

_NT = (((1,), (1,)), ((), ()))
_NN = (((1,), (0,)), ((), ()))


def _rms_scale(xf):
    return lax.rsqrt(jnp.mean(xf * xf, axis=-1, keepdims=True) + NORM_EPS)


def _adaln_kernel(c_ref, w_ref, b_ref, o_ref):
    c = c_ref[...]
    ca = c / (1.0 + jnp.exp(-c))
    o_ref[...] = lax.dot_general(ca.astype(jnp.bfloat16), w_ref[...].astype(jnp.bfloat16), _NN,
                                 preferred_element_type=jnp.float32) + b_ref[...]


def _adaln_call(c, w, b):
    bsz, d = c.shape
    n = w.shape[1]
    tn = 1024
    return pl.pallas_call(
        _adaln_kernel,
        grid=(n // tn,),
        in_specs=[pl.BlockSpec((bsz, d), lambda j: (0, 0)),
                  pl.BlockSpec((d, tn), lambda j: (0, j)),
                  pl.BlockSpec((1, tn), lambda j: (0, j))],
        out_specs=pl.BlockSpec((bsz, tn), lambda j: (0, j)),
        out_shape=jax.ShapeDtypeStruct((bsz, n), jnp.float32),
        compiler_params=pltpu.CompilerParams(dimension_semantics=("arbitrary",),
                                             vmem_limit_bytes=VMEM_LIMIT),
        name="adaln",
    )(c, w, b.reshape(1, n))


def _inproj_kernel(x_ref, mod_ref, g_ref, w_ref, cos_ref, sin_ref, o_ref):
    xf = x_ref[0]
    sh = mod_ref[0, 0:1, :]
    sc = mod_ref[0, 1:2, :]
    h = (xf * _rms_scale(xf) * g_ref[...]) * (1.0 + sc) + sh
    hb = h.astype(jnp.bfloat16)
    cos = cos_ref[...]
    sin = sin_ref[...]
    lane = lax.broadcasted_iota(jnp.int32, cos.shape, 1)
    first_half = (lane % HEAD_DIM) < (HEAD_DIM // 2)
    scale = HEAD_DIM ** -0.5
    gw = DIFF_WIDTH
    for n in range(6):
        acc = lax.dot_general(hb, w_ref[:, n * gw:(n + 1) * gw], _NN,
                              preferred_element_type=jnp.float32)
        if n < 2:
            for j in range(gw // LANES):
                xb = acc[:, j * LANES:(j + 1) * LANES]
                rot = jnp.where(first_half, pltpu.roll(xb, LANES - HEAD_DIM // 2, 1),
                                pltpu.roll(xb, HEAD_DIM // 2, 1))
                yb = xb * cos + rot * sin
                if n == 0:
                    yb = yb * (scale * LOG2E)
                o_ref[0, :, n * gw + j * LANES:n * gw + (j + 1) * LANES] = yb.astype(o_ref.dtype)
        else:
            if n == 3:
                acc = acc * (0.5 * scale)
            o_ref[0, :, n * gw:(n + 1) * gw] = acc.astype(o_ref.dtype)


def _inproj_call(x, mod3, g, w_bf16, cos128, sin128):
    bsz, seq, d = x.shape
    n = w_bf16.shape[1]
    tm = TOKEN_TILE
    const = dict(pipeline_mode=pl.Buffered(1))
    return pl.pallas_call(
        _inproj_kernel,
        grid=(bsz, seq // tm),
        in_specs=[pl.BlockSpec((1, tm, d), lambda b, t: (b, t, 0)),
                  pl.BlockSpec((1, N_MOD, d), lambda b, t: (b, 0, 0)),
                  pl.BlockSpec((1, d), lambda b, t: (0, 0)),
                  pl.BlockSpec((d, n), lambda b, t: (0, 0), **const),
                  pl.BlockSpec((tm, LANES), lambda b, t: (t, 0)),
                  pl.BlockSpec((tm, LANES), lambda b, t: (t, 0))],
        out_specs=pl.BlockSpec((1, tm, n), lambda b, t: (b, t, 0)),
        out_shape=jax.ShapeDtypeStruct((bsz, seq, n), jnp.bfloat16),
        compiler_params=pltpu.CompilerParams(dimension_semantics=("arbitrary", "arbitrary"),
                                             vmem_limit_bytes=VMEM_LIMIT),
        name="inproj",
    )(x, mod3, g.reshape(1, d), w_bf16, cos128, sin128)


def _attn_kernel(lq1_ref, lk1_ref, lq2_ref, lk2_ref, g_ref, dq_ref, dk_ref, dv_ref, sq_ref, sk_ref, sv_ref,
                 wo_ref, w1_ref, w2_ref, od_ref, os_ref, wo_bf_ref, w1_bf_ref, w2_bf_ref,
                 dvt_scr, dqm_scr, dacc_scr, m_scr, s_scr, p_scr, alpha_scr,
                 kp_scr, svt_scr, sqm_scr, sacc_scr, pre_scr, ap_scr, carry_scr, *, lambda_init):
    g = pl.program_id(1)
    nblk = dk_ref.shape[1]
    tb = ATT_BLOCK
    tw = 2 * tb
    dv = 2 * HEAD_DIM
    vrows = dv + ONES_ROWS
    nd = N_DIFF_HEADS
    ns = N_SB_HEADS
    d_last = Q_GROUP * nd - 1

    wo_bf_ref[...] = wo_ref[...].astype(wo_bf_ref.dtype)
    w1_bf_ref[...] = w1_ref[...].astype(w1_bf_ref.dtype)
    w2_bf_ref[...] = w2_ref[...].astype(w2_bf_ref.dtype)

    @pl.when(g == 0)
    def _():
        for blk in range(nblk):
            vt = dv_ref[0, blk].astype(jnp.float32).T.astype(jnp.bfloat16)
            for hd in range(nd):
                dvt_scr[blk, hd * vrows:hd * vrows + dv, :] = vt[hd * dv:(hd + 1) * dv, :]
                dvt_scr[blk, hd * vrows + dv:(hd + 1) * vrows, :] = jnp.ones((ONES_ROWS, tb), jnp.bfloat16)
        row = lax.broadcasted_iota(jnp.int32, (tb, tb), 0)
        col = lax.broadcasted_iota(jnp.int32, (tb, tb), 1)
        pos = SEG_LEN * (row % SUBLANES) + (SEG_LEN - 1 - row // SUBLANES)
        perm = jnp.where(col == pos, 1.0, 0.0).astype(jnp.bfloat16)
        for blk in range(nblk):
            kp = lax.dot_general(perm, sk_ref[0, blk], _NN, preferred_element_type=jnp.float32)
            kp_scr[blk] = kp.astype(jnp.bfloat16)
            vp = lax.dot_general(perm, sv_ref[0, blk], _NN, preferred_element_type=jnp.float32)
            svt_scr[blk] = vp.T.astype(jnp.bfloat16)

    lane = lax.broadcasted_iota(jnp.int32, (tb, LANES), 1)
    for qb in range(Q_GROUP):
        for c in range(nd):
            for ref, scr in ((dq_ref, dqm_scr), (sq_ref, sqm_scr)):
                qp = ref[0, qb, :, c * LANES:(c + 1) * LANES]
                zero = jnp.zeros_like(qp)
                scr[qb * nd + c, :tb, :] = jnp.where(lane < HEAD_DIM, qp, zero)
                scr[qb * nd + c, tb:, :] = jnp.where(lane >= HEAD_DIM, qp, zero)
        for h in range(ns):
            carry_scr[qb * ns + h] = jnp.ones((SUBLANES, tb), jnp.float32)
    dacc_scr[d_last] = jnp.zeros((vrows, tw), jnp.float32)
    alpha_scr[...] = jnp.zeros((1, tw), jnp.float32)

    sub = lax.broadcasted_iota(jnp.int32, (SUBLANES, tb), 0)
    qcol = lax.broadcasted_iota(jnp.int32, (SUBLANES, tb), 1)
    qcol2 = lax.broadcasted_iota(jnp.int32, (SUBLANES, tw), 1) % tb

    def d_scores(j, qb, hd):
        return lax.dot_general(dk_ref[0, j, :, hd * LANES:(hd + 1) * LANES], dqm_scr[qb * nd + hd], _NT,
                               preferred_element_type=jnp.float32)

    def d_elementwise(s_t, qb, hd, first):
        buf = hd % 2
        it = qb * nd + hd
        mx = None
        for v in range(tb // SUBLANES):
            sv = s_t[v * SUBLANES:(v + 1) * SUBLANES, :]
            if first:
                sv = jnp.where(qcol2 >= (v * SUBLANES // CHUNK) * CHUNK, sv, -jnp.inf)
            mx = sv if mx is None else jnp.maximum(mx, sv)
            s_scr[buf, v * SUBLANES:(v + 1) * SUBLANES, :] = sv
        tmax = jnp.max(mx, axis=0, keepdims=True)
        if first:
            m_new = tmax
            alpha = None
        else:
            m_old = m_scr[it]
            m_new = jnp.maximum(m_old, tmax)
            alpha = jnp.exp2(m_old - m_new)
        m_scr[it] = m_new
        return jnp.exp2(s_scr[buf] - m_new).astype(jnp.bfloat16), alpha

    def d_pv(j, hd, p):
        return lax.dot_general(dvt_scr[j, hd * vrows:(hd + 1) * vrows, :], p, _NN,
                               preferred_element_type=jnp.float32)

    def finish_deferred(j_prev):
        dacc_scr[d_last] = alpha_scr[...] * dacc_scr[d_last] + d_pv(j_prev, nd - 1, p_scr[...])

    def s_scores(j, qb, pair):
        return lax.dot_general(kp_scr[j, :, pair * LANES:(pair + 1) * LANES], sqm_scr[qb * N_PAIRS + pair], _NT,
                               preferred_element_type=jnp.float32)

    def s_elementwise(zh, qb, h, diag):
        buf = h % 2
        rem = jnp.ones((SUBLANES, tb), jnp.float32)
        prev = None
        for v in range(SEG_LEN):
            beta = 0.5 * jnp.tanh(zh[v * SUBLANES:(v + 1) * SUBLANES, :]) + 0.5
            if diag:
                beta = jnp.where(SEG_LEN * sub + (SEG_LEN - 1 - v) < qcol, beta, 0.0)
            a_v = beta * rem
            rem = rem - a_v
            if v % 2 == 0:
                prev = a_v
            else:
                ap_scr[buf, (v - 1) * SUBLANES:(v + 1) * SUBLANES, :] = jnp.concatenate(
                    [prev, a_v], axis=0).astype(jnp.bfloat16)
        incl = rem
        for k in (1, 2, 4):
            shifted = pltpu.roll(incl, SUBLANES - k, 0)
            incl = incl * jnp.where(sub + k < SUBLANES, shifted, 1.0)
        excl = jnp.where(sub + 1 < SUBLANES, pltpu.roll(incl, SUBLANES - 1, 0), 1.0)
        carry = carry_scr[qb * ns + h]
        start = carry * excl
        carry_scr[qb * ns + h] = carry * jnp.broadcast_to(incl[0:1, :], carry.shape)
        start2 = jnp.concatenate([start, start], axis=0).astype(jnp.bfloat16)
        return jnp.concatenate(
            [ap_scr[buf, u * 2 * SUBLANES:(u + 1) * 2 * SUBLANES, :] * start2 for u in range(SEG_LEN // 2)],
            axis=0)

    def s_av(j, h, a):
        return lax.dot_general(svt_scr[j, h * HEAD_DIM:(h + 1) * HEAD_DIM, :], a, _NN,
                               preferred_element_type=jnp.float32)

    def tile_body(j, blocks, nxt, j_prev):
        if j_prev is not None:
            finish_deferred(j_prev)
        entries = []
        for qb, masked in blocks:
            for c in range(nd):
                entries.append(("s", qb, c, masked))
                entries.append(("d", qb, c, masked))
        sc = [None] * len(entries)
        sc[0] = pre_scr[...]
        issued = 1

        def issue(upto):
            nonlocal issued
            while issued <= upto:
                if issued < len(entries):
                    kind, qb, c, _ = entries[issued]
                    sc[issued] = s_scores(j, qb, c) if kind == "s" else d_scores(j, qb, c)
                elif issued == len(entries):
                    pre_scr[...] = s_scores(nxt[0], nxt[1], 0)
                issued += 1

        for k, (kind, qb, c, masked) in enumerate(entries):
            issue(k + 2)
            if kind == "s":
                for hh in range(2):
                    h = 2 * c + hh
                    a = s_elementwise(sc[k][:, hh * tb:(hh + 1) * tb], qb, h, masked)
                    rows = slice(h * HEAD_DIM, (h + 1) * HEAD_DIM)
                    if masked:
                        sacc_scr[qb, rows, :] = s_av(j, h, a)
                    else:
                        sacc_scr[qb, rows, :] += s_av(j, h, a)
            else:
                p, alpha = d_elementwise(sc[k], qb, c, masked)
                it = qb * nd + c
                if it == d_last:
                    p_scr[...] = p
                    if not masked:
                        alpha_scr[...] = alpha
                elif masked:
                    dacc_scr[it] = d_pv(j, c, p)
                else:
                    dacc_scr[it] = alpha * dacc_scr[it] + d_pv(j, c, p)
            sc[k] = None

    base = g * Q_GROUP
    pre_scr[...] = s_scores(base + Q_GROUP - 1, Q_GROUP - 1, 0)
    j_prev = None
    for d in range(Q_GROUP - 1, -1, -1):
        blocks = [(d, True)] + [(qb, False) for qb in range(d + 1, Q_GROUP)]
        nxt = (base + d - 1, d - 1) if d > 0 else (jnp.maximum(base - 1, 0), 0)
        tile_body(base + d, blocks, nxt, j_prev)
        j_prev = base + d

    def body(t, c):
        j = base - 1 - t
        tile_body(j, [(qb, False) for qb in range(Q_GROUP)], (jnp.maximum(j - 1, 0), 0), j + 1)
        return c

    if nblk > Q_GROUP:
        lax.fori_loop(0, base, body, 0)
    finish_deferred(0)

    lam = (jnp.exp(jnp.sum(lq1_ref[...] * lk1_ref[...], axis=-1, keepdims=True))
           - jnp.exp(jnp.sum(lq2_ref[...] * lk2_ref[...], axis=-1, keepdims=True)) + lambda_init)
    for qb in range(Q_GROUP):
        for hd in range(nd):
            a1 = dacc_scr[qb * nd + hd, :, :tb]
            a2 = dacc_scr[qb * nd + hd, :, tb:]
            o_t = a1[:dv] / a1[dv:dv + 1] - lam * (a2[:dv] / a2[dv:dv + 1])
            ms = jnp.mean(o_t * o_t, axis=0, keepdims=True)
            o_t = o_t * lax.rsqrt(ms + NORM_EPS) * g_ref[...] * (1.0 - lambda_init)
            od_ref[0, qb * tb:(qb + 1) * tb, hd * LANES:(hd + 1) * LANES] = o_t.T.astype(od_ref.dtype)
        os_ref[0, qb * tb:(qb + 1) * tb, :] = sacc_scr[qb].T.astype(os_ref.dtype)


def _attn_call(proj4, lq1, lk1, lq2, lk2, subln_g, lambda_init, w_out, w_ff1, w_ff2):
    bsz, nblk, tb, _ = proj4.shape
    seq = nblk * tb
    w = DIFF_WIDTH
    assert SB_WIDTH == w and nblk % Q_GROUP == 0
    gsteps = nblk // Q_GROUP
    nsteps = bsz * gsteps

    def slab(arr):
        rows = arr.shape[0] // nsteps
        assert rows * nsteps == arr.shape[0] and rows % (2 * SUBLANES) == 0
        return pl.BlockSpec((rows, arr.shape[1]), lambda b, g: (b * gsteps + g, 0))

    wspecs = [slab(w_out), slab(w_ff1), slab(w_ff2)]
    wshapes = [jax.ShapeDtypeStruct(a.shape, jnp.bfloat16) for a in (w_out, w_ff1, w_ff2)]
    vrows = 2 * HEAD_DIM + ONES_ROWS
    nit = Q_GROUP * N_DIFF_HEADS
    vec = lambda a: a.reshape(1, HEAD_DIM)
    small = pl.BlockSpec((1, HEAD_DIM), lambda b, g: (0, 0))
    qspec = lambda col: pl.BlockSpec((1, Q_GROUP, tb, w), lambda b, g: (b, g, 0, col))
    kvspec = lambda col: pl.BlockSpec((1, nblk, tb, w), lambda b, g: (b, 0, 0, col))
    ospec = pl.BlockSpec((1, Q_GROUP * tb, w), lambda b, g: (b, g, 0))
    oshape = jax.ShapeDtypeStruct((bsz, seq, w), jnp.bfloat16)
    return pl.pallas_call(
        functools.partial(_attn_kernel, lambda_init=lambda_init),
        grid=(bsz, nblk // Q_GROUP),
        in_specs=[small, small, small, small,
                  pl.BlockSpec((2 * HEAD_DIM, 1), lambda b, g: (0, 0)),
                  qspec(0), kvspec(1), kvspec(2), qspec(3), kvspec(4), kvspec(5)] + wspecs,
        out_specs=[ospec, ospec] + wspecs,
        out_shape=[oshape, oshape] + wshapes,
        scratch_shapes=[pltpu.VMEM((nblk, N_DIFF_HEADS * vrows, tb), jnp.bfloat16),
                        pltpu.VMEM((nit, 2 * tb, LANES), jnp.bfloat16),
                        pltpu.VMEM((nit, vrows, 2 * tb), jnp.float32),
                        pltpu.VMEM((nit, 1, 2 * tb), jnp.float32),
                        pltpu.VMEM((2, tb, 2 * tb), jnp.float32),
                        pltpu.VMEM((tb, 2 * tb), jnp.bfloat16),
                        pltpu.VMEM((1, 2 * tb), jnp.float32),
                        pltpu.VMEM((nblk, tb, w), jnp.bfloat16),
                        pltpu.VMEM((nblk, w, tb), jnp.bfloat16),
                        pltpu.VMEM((Q_GROUP * N_PAIRS, 2 * tb, LANES), jnp.bfloat16),
                        pltpu.VMEM((Q_GROUP, w, tb), jnp.float32),
                        pltpu.VMEM((tb, 2 * tb), jnp.float32),
                        pltpu.VMEM((2, tb, tb), jnp.bfloat16),
                        pltpu.VMEM((Q_GROUP * N_SB_HEADS, SUBLANES, tb), jnp.float32)],
        compiler_params=pltpu.CompilerParams(
            dimension_semantics=("arbitrary", "arbitrary"), vmem_limit_bytes=VMEM_LIMIT),
        name="attn",
    )(vec(lq1), vec(lk1), vec(lq2), vec(lk2), subln_g.reshape(2 * HEAD_DIM, 1),
      proj4, proj4, proj4, proj4, proj4, proj4, w_out, w_ff1, w_ff2)


def _out_ffn_kernel(x_ref, od_ref, os_ref, mod_ref, wo_ref, gf_ref, w1_ref, w2_ref, gl_ref, o_ref):
    g_m = mod_ref[0, 2:3, :]
    sh_f = mod_ref[0, 3:4, :]
    sc_f = mod_ref[0, 4:5, :]
    g_f = mod_ref[0, 5:6, :]
    d_ff = w1_ref.shape[1]
    fc = 1024
    tm = x_ref.shape[1]
    halves = [slice(k * (tm // 2), (k + 1) * (tm // 2)) for k in range(2)]
    x1, h, f = [None, None], [None, None], [None, None]
    for k, rows in enumerate(halves):
        mixed = (lax.dot_general(od_ref[0, rows, :], wo_ref[:DIFF_WIDTH, :], _NN, preferred_element_type=jnp.float32)
                 + lax.dot_general(os_ref[0, rows, :], wo_ref[DIFF_WIDTH:, :], _NN,
                                   preferred_element_type=jnp.float32))
        x1[k] = x_ref[0, rows, :] + g_m * mixed
    for k in range(2):
        h[k] = ((x1[k] * _rms_scale(x1[k]) * gf_ref[...]) * (1.0 + sc_f) + sh_f).astype(jnp.bfloat16)
    for c in range(d_ff // fc):
        for k in range(2):
            u = lax.dot_general(h[k], w1_ref[:, c * fc:(c + 1) * fc], _NN, preferred_element_type=jnp.float32)
            r = jnp.square(jnp.maximum(u, 0.0)).astype(jnp.bfloat16)
            part = lax.dot_general(r, w2_ref[c * fc:(c + 1) * fc, :], _NN, preferred_element_type=jnp.float32)
            f[k] = part if f[k] is None else f[k] + part
    for k, rows in enumerate(halves):
        x2 = x1[k] + g_f * f[k]
        o_ref[0, rows, :] = x2 * _rms_scale(x2) * gl_ref[...]


def _out_ffn_call(x, o_diff, o_sb, mod3, wo, gf, w1, w2, gl):
    bsz, seq, d = x.shape
    tm = TOKEN_TILE
    const = dict(pipeline_mode=pl.Buffered(1))
    tok = lambda w: pl.BlockSpec((1, tm, w), lambda b, t: (b, t, 0))
    return pl.pallas_call(
        _out_ffn_kernel,
        grid=(bsz, seq // tm),
        in_specs=[tok(d), tok(DIFF_WIDTH), tok(SB_WIDTH),
                  pl.BlockSpec((1, N_MOD, d), lambda b, t: (b, 0, 0)),
                  pl.BlockSpec(wo.shape, lambda b, t: (0, 0), **const),
                  pl.BlockSpec((1, d), lambda b, t: (0, 0)),
                  pl.BlockSpec(w1.shape, lambda b, t: (0, 0), **const),
                  pl.BlockSpec(w2.shape, lambda b, t: (0, 0), **const),
                  pl.BlockSpec((1, d), lambda b, t: (0, 0))],
        out_specs=tok(d),
        out_shape=jax.ShapeDtypeStruct((bsz, seq, d), jnp.float32),
        compiler_params=pltpu.CompilerParams(dimension_semantics=("arbitrary", "arbitrary"),
                                             vmem_limit_bytes=VMEM_LIMIT),
        name="out_ffn",
    )(x, o_diff, o_sb, mod3, wo, gf.reshape(1, d), w1, w2, gl.reshape(1, d))


def _rope_tables(seq_len):
    dim = HEAD_DIM
    inv = 1.0 / (ROPE_THETA ** (jnp.arange(0, dim, 2, dtype=jnp.float32) / dim))
    ang = jnp.arange(seq_len, dtype=jnp.float32)[:, None] * inv[None, :]
    ang = jnp.concatenate([ang, ang], axis=-1)
    cos, sin = jnp.cos(ang), jnp.sin(ang)
    sign = jnp.where(jnp.arange(dim) < dim // 2, -1.0, 1.0).astype(jnp.float32)
    reps = LANES // dim
    return jnp.tile(cos, (1, reps)), jnp.tile(sin * sign[None, :], (1, reps))


def kernel(x, c, ada_w, ada_b, mix_norm_g, w_in, lambda_q1, lambda_k1, lambda_q2, lambda_k2,
           diff_subln_g, w_out, ffn_norm_g, w_ff1, w_ff2, final_norm_g):
    bsz, seq, d = x.shape
    depth = ada_w.shape[0]
    assert depth == 1, "the fused final norm assumes a single layer"
    assert seq % TOKEN_TILE == 0 and seq % ATT_BLOCK == 0
    cos128, sin128 = _rope_tables(seq)
    bf = jnp.bfloat16
    layer = 0
    lambda_init = 0.8 - 0.6 * math.exp(-0.3 * layer)
    mod3 = _adaln_call(c, ada_w[layer], ada_b[layer]).reshape(bsz, N_MOD, d)
    proj = _inproj_call(x, mod3, mix_norm_g[layer], w_in[layer].astype(bf), cos128, sin128)
    proj4 = proj.reshape(bsz, seq // ATT_BLOCK, ATT_BLOCK, proj.shape[-1])
    o_diff, o_sb, wo_bf, w1_bf, w2_bf = _attn_call(
        proj4, lambda_q1[layer], lambda_k1[layer], lambda_q2[layer], lambda_k2[layer], diff_subln_g[layer],
        lambda_init, w_out[layer], w_ff1[layer], w_ff2[layer])
    return _out_ffn_call(x, o_diff, o_sb, mod3, wo_bf, ffn_norm_g[layer], w1_bf, w2_bf, final_norm_g)
```

```python
import functools
import math

import jax
import jax.numpy as jnp
import numpy as np
from jax import lax
from jax.experimental import pallas as pl
from jax.experimental.pallas import tpu as pltpu

HEAD_DIM = 64
N_DIFF_HEADS = 4
N_SB_HEADS = 8
DIFF_WIDTH = N_DIFF_HEADS * 2 * HEAD_DIM
SB_WIDTH = N_SB_HEADS * HEAD_DIM
CHUNK = 64
ROPE_THETA = 10000.0
NORM_EPS = 1e-6
N_MOD = 6

LANES = 128
SUBLANES = 8
ATT_BLOCK = 256
SEG_LEN = ATT_BLOCK // SUBLANES
TOKEN_TILE = 512
N_PAIRS = N_SB_HEADS // 2
Q_GROUP = 4
ONES_ROWS = 16
LOG2E = 1.4426950408889634
VMEM_LIMIT = 56 * 1024 * 1024

_NT = (((1,), (1,)), ((), ()))
_NN = (((1,), (0,)), ((), ()))


def _rms_scale(xf):
    return lax.rsqrt(jnp.mean(xf * xf, axis=-1, keepdims=True) + NORM_EPS)


def _adaln_kernel(c_ref, w_ref, b_ref, o_ref):
    c = c_ref[...]
    ca = c / (1.0 + jnp.exp(-c))
    o_ref[0] = lax.dot_general(ca.astype(jnp.bfloat16), w_ref[...].astype(jnp.bfloat16), _NN,
                               preferred_element_type=jnp.float32) + b_ref[...]


def _adaln_call(c, w, b):
    bsz, d = c.shape
    n = w.shape[1]
    tn = d
    return pl.pallas_call(
        _adaln_kernel,
        grid=(n // tn,),
        in_specs=[pl.BlockSpec((bsz, d), lambda j: (0, 0)),
                  pl.BlockSpec((d, tn), lambda j: (0, j)),
                  pl.BlockSpec((1, tn), lambda j: (0, j))],
        out_specs=pl.BlockSpec((1, bsz, tn), lambda j: (j, 0, 0)),
        out_shape=jax.ShapeDtypeStruct((n // tn, bsz, tn), jnp.float32),
        compiler_params=pltpu.CompilerParams(dimension_semantics=("arbitrary",),
                                             vmem_limit_bytes=VMEM_LIMIT),
        name="adaln",
    )(c, w, b.reshape(1, n))


def _inproj_kernel(x_ref, mod_ref, g_ref, w_ref, cos_ref, sin_ref, o_ref, wb_scr):
    xf = x_ref[0]
    b = pl.program_id(0)
    sh = mod_ref[0, pl.ds(b, 1), :]
    sc = mod_ref[1, pl.ds(b, 1), :]
    h = (xf * _rms_scale(xf) * g_ref[...]) * (1.0 + sc) + sh
    hb = h.astype(jnp.bfloat16)
    gw = DIFF_WIDTH

    @pl.when((b == 0) & (pl.program_id(1) == 0))
    def _():
        for n in range(w_ref.shape[1] // gw):
            wb_scr[:, n * gw:(n + 1) * gw] = w_ref[:, n * gw:(n + 1) * gw].astype(wb_scr.dtype)

    cos = cos_ref[...]
    sin = sin_ref[...]
    lane = lax.broadcasted_iota(jnp.int32, cos.shape, 1)
    first_half = (lane % HEAD_DIM) < (HEAD_DIM // 2)
    scale = HEAD_DIM ** -0.5
    for n in range(6):
        acc = lax.dot_general(hb, wb_scr[:, n * gw:(n + 1) * gw], _NN,
                              preferred_element_type=jnp.float32)
        if n < 2:
            for j in range(gw // LANES):
                xb = acc[:, j * LANES:(j + 1) * LANES]
                rot = jnp.where(first_half, pltpu.roll(xb, LANES - HEAD_DIM // 2, 1),
                                pltpu.roll(xb, HEAD_DIM // 2, 1))
                yb = xb * cos + rot * sin
                if n == 0:
                    yb = yb * (scale * LOG2E)
                o_ref[0, :, n * gw + j * LANES:n * gw + (j + 1) * LANES] = yb.astype(o_ref.dtype)
        else:
            if n == 3:
                acc = acc * (0.5 * scale)
            o_ref[0, :, n * gw:(n + 1) * gw] = acc.astype(o_ref.dtype)


def _inproj_call(x, mod3, g, w, cos128, sin128):
    bsz, seq, d = x.shape
    n = w.shape[1]
    tm = TOKEN_TILE
    const = dict(pipeline_mode=pl.Buffered(1))
    return pl.pallas_call(
        _inproj_kernel,
        grid=(bsz, seq // tm),
        in_specs=[pl.BlockSpec((1, tm, d), lambda b, t: (b, t, 0)),
                  pl.BlockSpec((N_MOD, bsz, d), lambda b, t: (0, 0, 0)),
                  pl.BlockSpec((1, d), lambda b, t: (0, 0)),
                  pl.BlockSpec((d, n), lambda b, t: (0, 0), **const),
                  pl.BlockSpec((tm, LANES), lambda b, t: (t, 0)),
                  pl.BlockSpec((tm, LANES), lambda b, t: (t, 0))],
        out_specs=pl.BlockSpec((1, tm, n), lambda b, t: (b, t, 0)),
        out_shape=jax.ShapeDtypeStruct((bsz, seq, n), jnp.bfloat16),
        scratch_shapes=[pltpu.VMEM((d, n), jnp.bfloat16)],
        compiler_params=pltpu.CompilerParams(dimension_semantics=("arbitrary", "arbitrary"),
                                             vmem_limit_bytes=VMEM_LIMIT),
        name="inproj",
    )(x, mod3, g.reshape(1, d), w, cos128, sin128)


def _attn_kernel(lq1_ref, lk1_ref, lq2_ref, lk2_ref, g_ref, dq_ref, dk_ref, dv_ref, sq_ref, sk_ref, sv_ref,
                 wo_ref, w1_ref, w2_ref, od_ref, os_ref, wo_bf_ref, w1_bf_ref, w2_bf_ref,
                 dvt_scr, dqm_scr, dacc_scr, m_scr, s_scr, p_scr, alpha_scr,
                 kp_scr, svt_scr, sqm_scr, sacc_scr, pre_scr, ap_scr, carry_scr, *, lambda_init):
    g = pl.program_id(1)
    nblk = dk_ref.shape[1]
    tb = ATT_BLOCK
    tw = 2 * tb
    dv = 2 * HEAD_DIM
    vrows = dv + ONES_ROWS
    nd = N_DIFF_HEADS
    ns = N_SB_HEADS
    d_last = Q_GROUP * nd - 1

    wo_bf_ref[...] = wo_ref[...].astype(wo_bf_ref.dtype)
    w1_bf_ref[...] = w1_ref[...].astype(w1_bf_ref.dtype)
    w2_bf_ref[...] = w2_ref[...].astype(w2_bf_ref.dtype)

    @pl.when(g == 0)
    def _():
        for blk in range(nblk):
            vt = dv_ref[0, blk].astype(jnp.float32).T.astype(jnp.bfloat16)
            for hd in range(nd):
                dvt_scr[blk, hd * vrows:hd * vrows + dv, :] = vt[hd * dv:(hd + 1) * dv, :]
                dvt_scr[blk, hd * vrows + dv:(hd + 1) * vrows, :] = jnp.ones((ONES_ROWS, tb), jnp.bfloat16)
        row = lax.broadcasted_iota(jnp.int32, (tb, tb), 0)
        col = lax.broadcasted_iota(jnp.int32, (tb, tb), 1)
        pos = SEG_LEN * (row % SUBLANES) + (SEG_LEN - 1 - row // SUBLANES)
        perm = jnp.where(col == pos, 1.0, 0.0).astype(jnp.bfloat16)
        for blk in range(nblk):
            kp = lax.dot_general(perm, sk_ref[0, blk], _NN, preferred_element_type=jnp.float32)
            kp_scr[blk] = kp.astype(jnp.bfloat16)
            vp = lax.dot_general(perm, sv_ref[0, blk], _NN, preferred_element_type=jnp.float32)
            svt_scr[blk] = vp.T.astype(jnp.bfloat16)

    lane = lax.broadcasted_iota(jnp.int32, (tb, LANES), 1)
    for qb in range(Q_GROUP):
        for c in range(nd):
            for ref, scr in ((dq_ref, dqm_scr), (sq_ref, sqm_scr)):
                qp = ref[0, qb, :, c * LANES:(c + 1) * LANES]
                zero = jnp.zeros_like(qp)
                scr[qb * nd + c, :tb, :] = jnp.where(lane < HEAD_DIM, qp, zero)
                scr[qb * nd + c, tb:, :] = jnp.where(lane >= HEAD_DIM, qp, zero)
        for h in range(ns):
            carry_scr[qb * ns + h] = jnp.ones((SUBLANES, tb), jnp.float32)
    dacc_scr[d_last] = jnp.zeros((vrows, tw), jnp.float32)
    alpha_scr[...] = jnp.zeros((1, tw), jnp.float32)

    sub = lax.broadcasted_iota(jnp.int32, (SUBLANES, tb), 0)
    qcol = lax.broadcasted_iota(jnp.int32, (SUBLANES, tb), 1)
    qcol2 = lax.broadcasted_iota(jnp.int32, (SUBLANES, tw), 1) % tb

    def d_scores(j, qb, hd):
        return lax.dot_general(dk_ref[0, j, :, hd * LANES:(hd + 1) * LANES], dqm_scr[qb * nd + hd], _NT,
                               preferred_element_type=jnp.float32)

    def d_elementwise(s_t, qb, hd, first):
        buf = hd % 2
        it = qb * nd + hd
        mx = None
        for v in range(tb // SUBLANES):
            sv = s_t[v * SUBLANES:(v + 1) * SUBLANES, :]
            if first:
                sv = jnp.where(qcol2 >= (v * SUBLANES // CHUNK) * CHUNK, sv, -jnp.inf)
            mx = sv if mx is None else jnp.maximum(mx, sv)
            s_scr[buf, v * SUBLANES:(v + 1) * SUBLANES, :] = sv
        tmax = jnp.max(mx, axis=0, keepdims=True)
        if first:
            m_new = tmax
            alpha = None
        else:
            m_old = m_scr[it]
            m_new = jnp.maximum(m_old, tmax)
            alpha = jnp.exp2(m_old - m_new)
        m_scr[it] = m_new
        return jnp.exp2(s_scr[buf] - m_new).astype(jnp.bfloat16), alpha

    def d_pv(j, hd, p):
        return lax.dot_general(dvt_scr[j, hd * vrows:(hd + 1) * vrows, :], p, _NN,
                               preferred_element_type=jnp.float32)

    def finish_deferred(j_prev):
        dacc_scr[d_last] = alpha_scr[...] * dacc_scr[d_last] + d_pv(j_prev, nd - 1, p_scr[...])

    def s_scores(j, qb, pair):
        return lax.dot_general(kp_scr[j, :, pair * LANES:(pair + 1) * LANES], sqm_scr[qb * N_PAIRS + pair], _NT,
                               preferred_element_type=jnp.float32)

    def s_elementwise(zh, qb, h, diag):
        buf = h % 2
        rem = jnp.ones((SUBLANES, tb), jnp.float32)
        prev = None
        for v in range(SEG_LEN):
            beta = 0.5 * jnp.tanh(zh[v * SUBLANES:(v + 1) * SUBLANES, :]) + 0.5
            if diag:
                beta = jnp.where(SEG_LEN * sub + (SEG_LEN - 1 - v) < qcol, beta, 0.0)
            a_v = beta * rem
            rem = rem - a_v
            if v % 2 == 0:
                prev = a_v
            else:
                ap_scr[buf, (v - 1) * SUBLANES:(v + 1) * SUBLANES, :] = jnp.concatenate(
                    [prev, a_v], axis=0).astype(jnp.bfloat16)
        incl = rem
        for k in (1, 2, 4):
            shifted = pltpu.roll(incl, SUBLANES - k, 0)
            incl = incl * jnp.where(sub + k < SUBLANES, shifted, 1.0)
        excl = jnp.where(sub + 1 < SUBLANES, pltpu.roll(incl, SUBLANES - 1, 0), 1.0)
        carry = carry_scr[qb * ns + h]
        start = carry * excl
        carry_scr[qb * ns + h] = carry * jnp.broadcast_to(incl[0:1, :], carry.shape)
        start2 = jnp.concatenate([start, start], axis=0).astype(jnp.bfloat16)
        return jnp.concatenate(
            [ap_scr[buf, u * 2 * SUBLANES:(u + 1) * 2 * SUBLANES, :] * start2 for u in range(SEG_LEN // 2)],
            axis=0)

    def s_av(j, h, a):
        return lax.dot_general(svt_scr[j, h * HEAD_DIM:(h + 1) * HEAD_DIM, :], a, _NN,
                               preferred_element_type=jnp.float32)

    def tile_body(j, blocks, nxt, j_prev):
        if j_prev is not None:
            finish_deferred(j_prev)
        entries = []
        for qb, masked in blocks:
            for c in range(nd):
                entries.append(("s", qb, c, masked))
                entries.append(("d", qb, c, masked))
        sc = [None] * len(entries)
        sc[0] = pre_scr[...]
        issued = 1

        def issue(upto):
            nonlocal issued
            while issued <= upto:
                if issued < len(entries):
                    kind, qb, c, _ = entries[issued]
                    sc[issued] = s_scores(j, qb, c) if kind == "s" else d_scores(j, qb, c)
                elif issued == len(entries):
                    pre_scr[...] = s_scores(nxt[0], nxt[1], 0)
                issued += 1

        for k, (kind, qb, c, masked) in enumerate(entries):
            issue(k + 2)
            if kind == "s":
                for hh in range(2):
                    h = 2 * c + hh
                    a = s_elementwise(sc[k][:, hh * tb:(hh + 1) * tb], qb, h, masked)
                    rows = slice(h * HEAD_DIM, (h + 1) * HEAD_DIM)
                    if masked:
                        sacc_scr[qb, rows, :] = s_av(j, h, a)
                    else:
                        sacc_scr[qb, rows, :] += s_av(j, h, a)
            else:
                p, alpha = d_elementwise(sc[k], qb, c, masked)
                it = qb * nd + c
                if it == d_last:
                    p_scr[...] = p
                    if not masked:
                        alpha_scr[...] = alpha
                elif masked:
                    dacc_scr[it] = d_pv(j, c, p)
                else:
                    dacc_scr[it] = alpha * dacc_scr[it] + d_pv(j, c, p)
            sc[k] = None

    base = g * Q_GROUP
    pre_scr[...] = s_scores(base + Q_GROUP - 1, Q_GROUP - 1, 0)
    j_prev = None
    for d in range(Q_GROUP - 1, -1, -1):
        blocks = [(d, True)] + [(qb, False) for qb in range(d + 1, Q_GROUP)]
        nxt = (base + d - 1, d - 1) if d > 0 else (jnp.maximum(base - 1, 0), 0)
        tile_body(base + d, blocks, nxt, j_prev)
        j_prev = base + d

    def body(t, c):
        j = base - 1 - t
        tile_body(j, [(qb, False) for qb in range(Q_GROUP)], (jnp.maximum(j - 1, 0), 0), j + 1)
        return c

    if nblk > Q_GROUP:
        lax.fori_loop(0, base, body, 0)
    finish_deferred(0)

    g_col = jnp.broadcast_to(g_ref[...], (dv, dv)).T
    g_col = jnp.concatenate([g_col] * (tb // dv), axis=1)
    lam = (jnp.exp(jnp.sum(lq1_ref[...] * lk1_ref[...], axis=-1, keepdims=True))
           - jnp.exp(jnp.sum(lq2_ref[...] * lk2_ref[...], axis=-1, keepdims=True)) + lambda_init)
    for qb in range(Q_GROUP):
        for hd in range(nd):
            a1 = dacc_scr[qb * nd + hd, :, :tb]
            a2 = dacc_scr[qb * nd + hd, :, tb:]
            o_t = a1[:dv] / a1[dv:dv + 1] - lam * (a2[:dv] / a2[dv:dv + 1])
            ms = jnp.mean(o_t * o_t, axis=0, keepdims=True)
            o_t = o_t * lax.rsqrt(ms + NORM_EPS) * g_col * (1.0 - lambda_init)
            od_ref[0, qb * tb:(qb + 1) * tb, hd * LANES:(hd + 1) * LANES] = o_t.T.astype(od_ref.dtype)
        os_ref[0, qb * tb:(qb + 1) * tb, :] = sacc_scr[qb].T.astype(os_ref.dtype)


def _attn_call(proj4, lq1, lk1, lq2, lk2, subln_g, lambda_init, w_out, w_ff1, w_ff2):
    bsz, nblk, tb, _ = proj4.shape
    seq = nblk * tb
    w = DIFF_WIDTH
    assert SB_WIDTH == w and nblk % Q_GROUP == 0
    gsteps = nblk // Q_GROUP
    nsteps = bsz * gsteps

    def slab(arr):
        rows = arr.shape[0] // nsteps
        assert rows * nsteps == arr.shape[0] and rows % (2 * SUBLANES) == 0
        return pl.BlockSpec((rows, arr.shape[1]), lambda b, g: (b * gsteps + g, 0))

    wspecs = [slab(w_out), slab(w_ff1), slab(w_ff2)]
    wshapes = [jax.ShapeDtypeStruct(a.shape, jnp.bfloat16) for a in (w_out, w_ff1, w_ff2)]
    vrows = 2 * HEAD_DIM + ONES_ROWS
    nit = Q_GROUP * N_DIFF_HEADS
    vec = lambda a: a.reshape(1, HEAD_DIM)
    small = pl.BlockSpec((1, HEAD_DIM), lambda b, g: (0, 0))
    qspec = lambda col: pl.BlockSpec((1, Q_GROUP, tb, w), lambda b, g: (b, g, 0, col))
    kvspec = lambda col: pl.BlockSpec((1, nblk, tb, w), lambda b, g: (b, 0, 0, col))
    ospec = pl.BlockSpec((1, Q_GROUP * tb, w), lambda b, g: (b, g, 0))
    oshape = jax.ShapeDtypeStruct((bsz, seq, w), jnp.bfloat16)
    return pl.pallas_call(
        functools.partial(_attn_kernel, lambda_init=lambda_init),
        grid=(bsz, nblk // Q_GROUP),
        in_specs=[small, small, small, small,
                  pl.BlockSpec((1, 2 * HEAD_DIM), lambda b, g: (0, 0)),
                  qspec(0), kvspec(1), kvspec(2), qspec(3), kvspec(4), kvspec(5)] + wspecs,
        out_specs=[ospec, ospec] + wspecs,
        out_shape=[oshape, oshape] + wshapes,
        scratch_shapes=[pltpu.VMEM((nblk, N_DIFF_HEADS * vrows, tb), jnp.bfloat16),
                        pltpu.VMEM((nit, 2 * tb, LANES), jnp.bfloat16),
                        pltpu.VMEM((nit, vrows, 2 * tb), jnp.float32),
                        pltpu.VMEM((nit, 1, 2 * tb), jnp.float32),
                        pltpu.VMEM((2, tb, 2 * tb), jnp.float32),
                        pltpu.VMEM((tb, 2 * tb), jnp.bfloat16),
                        pltpu.VMEM((1, 2 * tb), jnp.float32),
                        pltpu.VMEM((nblk, tb, w), jnp.bfloat16),
                        pltpu.VMEM((nblk, w, tb), jnp.bfloat16),
                        pltpu.VMEM((Q_GROUP * N_PAIRS, 2 * tb, LANES), jnp.bfloat16),
                        pltpu.VMEM((Q_GROUP, w, tb), jnp.float32),
                        pltpu.VMEM((tb, 2 * tb), jnp.float32),
                        pltpu.VMEM((2, tb, tb), jnp.bfloat16),
                        pltpu.VMEM((Q_GROUP * N_SB_HEADS, SUBLANES, tb), jnp.float32)],
        compiler_params=pltpu.CompilerParams(
            dimension_semantics=("arbitrary", "arbitrary"), vmem_limit_bytes=VMEM_LIMIT),
        name="attn",
    )(vec(lq1), vec(lk1), vec(lq2), vec(lk2), subln_g.reshape(1, 2 * HEAD_DIM),
      proj4, proj4, proj4, proj4, proj4, proj4, w_out, w_ff1, w_ff2)


def _out_ffn_kernel(x_ref, od_ref, os_ref, mod_ref, wo_ref, gf_ref, w1_ref, w2_ref, gl_ref, o_ref):
    b = pl.program_id(0)
    g_m = mod_ref[2, pl.ds(b, 1), :]
    sh_f = mod_ref[3, pl.ds(b, 1), :]
    sc_f = mod_ref[4, pl.ds(b, 1), :]
    g_f = mod_ref[5, pl.ds(b, 1), :]
    d_ff = w1_ref.shape[1]
    fc = 1024
    tm = x_ref.shape[1]
    halves = [slice(k * (tm // 2), (k + 1) * (tm // 2)) for k in range(2)]
    x1, h, f = [None, None], [None, None], [None, None]
    for k, rows in enumerate(halves):
        mixed = (lax.dot_general(od_ref[0, rows, :], wo_ref[:DIFF_WIDTH, :], _NN, preferred_element_type=jnp.float32)
                 + lax.dot_general(os_ref[0, rows, :], wo_ref[DIFF_WIDTH:, :], _NN,
                                   preferred_element_type=jnp.float32))
        x1[k] = x_ref[0, rows, :] + g_m * mixed
    for k in range(2):
        h[k] = ((x1[k] * _rms_scale(x1[k]) * gf_ref[...]) * (1.0 + sc_f) + sh_f).astype(jnp.bfloat16)
    for c in range(d_ff // fc):
        for k in range(2):
            u = lax.dot_general(h[k], w1_ref[:, c * fc:(c + 1) * fc], _NN, preferred_element_type=jnp.float32)
            r = jnp.square(jnp.maximum(u, 0.0)).astype(jnp.bfloat16)
            part = lax.dot_general(r, w2_ref[c * fc:(c + 1) * fc, :], _NN, preferred_element_type=jnp.float32)
            f[k] = part if f[k] is None else f[k] + part
    for k, rows in enumerate(halves):
        x2 = x1[k] + g_f * f[k]
        o_ref[0, rows, :] = x2 * _rms_scale(x2) * gl_ref[...]


def _out_ffn_call(x, o_diff, o_sb, mod3, wo, gf, w1, w2, gl):
    bsz, seq, d = x.shape
    tm = TOKEN_TILE
    const = dict(pipeline_mode=pl.Buffered(1))
    tok = lambda w: pl.BlockSpec((1, tm, w), lambda b, t: (b, t, 0))
    return pl.pallas_call(
        _out_ffn_kernel,
        grid=(bsz, seq // tm),
        in_specs=[tok(d), tok(DIFF_WIDTH), tok(SB_WIDTH),
                  pl.BlockSpec((N_MOD, bsz, d), lambda b, t: (0, 0, 0)),
                  pl.BlockSpec(wo.shape, lambda b, t: (0, 0), **const),
                  pl.BlockSpec((1, d), lambda b, t: (0, 0)),
                  pl.BlockSpec(w1.shape, lambda b, t: (0, 0), **const),
                  pl.BlockSpec(w2.shape, lambda b, t: (0, 0), **const),
                  pl.BlockSpec((1, d), lambda b, t: (0, 0))],
        out_specs=tok(d),
        out_shape=jax.ShapeDtypeStruct((bsz, seq, d), jnp.float32),
        compiler_params=pltpu.CompilerParams(dimension_semantics=("arbitrary", "arbitrary"),
                                             vmem_limit_bytes=VMEM_LIMIT),
        name="out_ffn",
    )(x, o_diff, o_sb, mod3, wo, gf.reshape(1, d), w1, w2, gl.reshape(1, d))


def _rope_tables(seq_len):
    dim = HEAD_DIM
    inv = (1.0 / (np.float32(ROPE_THETA) ** (np.arange(0, dim, 2, dtype=np.float32) / np.float32(dim)))).astype(np.float32)
    ang = np.arange(seq_len, dtype=np.float32)[:, None] * inv[None, :]
    ang = np.concatenate([ang, ang], axis=-1)
    cos, sin = np.cos(ang).astype(np.float32), np.sin(ang).astype(np.float32)
    sign = np.where(np.arange(dim) < dim // 2, -1.0, 1.0).astype(np.float32)
    reps = LANES // dim
    return jnp.asarray(np.tile(cos, (1, reps))), jnp.asarray(np.tile(sin * sign[None, :], (1, reps)))


def kernel(x, c, ada_w, ada_b, mix_norm_g, w_in, lambda_q1, lambda_k1, lambda_q2, lambda_k2,
           diff_subln_g, w_out, ffn_norm_g, w_ff1, w_ff2, final_norm_g):
    bsz, seq, d = x.shape
    depth = ada_w.shape[0]
    assert depth == 1, "the fused final norm assumes a single layer"
    assert seq % TOKEN_TILE == 0 and seq % ATT_BLOCK == 0
    cos128, sin128 = _rope_tables(seq)
    layer = 0
    lambda_init = 0.8 - 0.6 * math.exp(-0.3 * layer)
    mod3 = _adaln_call(c, ada_w[layer], ada_b[layer])
    proj = _inproj_call(x, mod3, mix_norm_g[layer], w_in[layer], cos128, sin128)
    proj4 = proj.reshape(bsz, seq // ATT_BLOCK, ATT_BLOCK, proj.shape[-1])
    o_diff, o_sb, wo_bf, w1_bf, w2_bf = _attn_call(
        proj4, lambda_q1[layer], lambda_k1[layer], lambda_q2[layer], lambda_k2[layer], diff_subln_g[layer],
        lambda_init, w_out[layer], w_ff1[layer], w_ff2[layer])
    return _out_ffn_call(x, o_diff, o_sb, mod3, wo_bf, ffn_norm_g[layer], w1_bf, w2_bf, final_norm_g)
```

```python
import functools
import math

import jax
import jax.numpy as jnp
import numpy as np
from jax import lax
from jax.experimental import pallas as pl
from jax.experimental.pallas import tpu as pltpu

HEAD_DIM = 64
N_DIFF_HEADS = 4
N_SB_HEADS = 8
DIFF_WIDTH = N_DIFF_HEADS * 2 * HEAD_DIM
SB_WIDTH = N_SB_HEADS * HEAD_DIM
CHUNK = 64
ROPE_THETA = 10000.0
NORM_EPS = 1e-6
N_MOD = 6

LANES = 128
SUBLANES = 8
ATT_BLOCK = 256
SEG_LEN = ATT_BLOCK // SUBLANES
TOKEN_TILE = 512
N_PAIRS = N_SB_HEADS // 2
Q_GROUP = 4
ONES_ROWS = 16
LOG2E = 1.4426950408889634
VMEM_LIMIT = 56 * 1024 * 1024

_NT = (((1,), (1,)), ((), ()))
_NN = (((1,), (0,)), ((), ()))


def _rms_scale(xf):
    return lax.rsqrt(jnp.mean(xf * xf, axis=-1, keepdims=True) + NORM_EPS)


def _adaln_kernel(c_ref, w_ref, b_ref, o_ref):
    c = c_ref[...]
    ca = c / (1.0 + jnp.exp(-c))
    o_ref[0] = lax.dot_general(ca.astype(jnp.bfloat16), w_ref[...].astype(jnp.bfloat16), _NN,
                               preferred_element_type=jnp.float32) + b_ref[...]


def _adaln_call(c, w, b):
    bsz, d = c.shape
    n = w.shape[1]
    tn = d
    return pl.pallas_call(
        _adaln_kernel,
        grid=(n // tn,),
        in_specs=[pl.BlockSpec((bsz, d), lambda j: (0, 0)),
                  pl.BlockSpec((d, tn), lambda j: (0, j)),
                  pl.BlockSpec((1, tn), lambda j: (0, j))],
        out_specs=pl.BlockSpec((1, bsz, tn), lambda j: (j, 0, 0)),
        out_shape=jax.ShapeDtypeStruct((n // tn, bsz, tn), jnp.float32),
        compiler_params=pltpu.CompilerParams(dimension_semantics=("arbitrary",),
                                             vmem_limit_bytes=VMEM_LIMIT),
        name="adaln",
    )(c, w, b.reshape(1, n))


def _inproj_kernel(x_ref, mod_ref, g_ref, w_ref, cos_ref, sin_ref, o_ref, wb_scr):
    b = pl.program_id(0)
    gw = DIFF_WIDTH

    @pl.when((b == 0) & (pl.program_id(1) == 0))
    def _():
        for n in range(w_ref.shape[1] // gw):
            wb_scr[:, n * gw:(n + 1) * gw] = w_ref[:, n * gw:(n + 1) * gw].astype(wb_scr.dtype)

    xf = x_ref[0]
    sh = mod_ref[0, pl.ds(b, 1), :]
    sc = mod_ref[1, pl.ds(b, 1), :]
    h = (xf * _rms_scale(xf) * g_ref[...]) * (1.0 + sc) + sh
    hb = h.astype(jnp.bfloat16)
    cos = cos_ref[...]
    sin = sin_ref[...]
    lane = lax.broadcasted_iota(jnp.int32, cos.shape, 1)
    first_half = (lane % HEAD_DIM) < (HEAD_DIM // 2)
    scale = HEAD_DIM ** -0.5
    for n in range(6):
        acc = lax.dot_general(hb, wb_scr[:, n * gw:(n + 1) * gw], _NN,
                              preferred_element_type=jnp.float32)
        if n < 2:
            for j in range(gw // LANES):
                xb = acc[:, j * LANES:(j + 1) * LANES]
                rot = jnp.where(first_half, pltpu.roll(xb, LANES - HEAD_DIM // 2, 1),
                                pltpu.roll(xb, HEAD_DIM // 2, 1))
                yb = xb * cos + rot * sin
                if n == 0:
                    yb = yb * (scale * LOG2E)
                o_ref[0, :, n * gw + j * LANES:n * gw + (j + 1) * LANES] = yb.astype(o_ref.dtype)
        else:
            if n == 3:
                acc = acc * (0.5 * scale)
            o_ref[0, :, n * gw:(n + 1) * gw] = acc.astype(o_ref.dtype)


def _inproj_call(x, mod3, g, w, cos128, sin128):
    bsz, seq, d = x.shape
    n = w.shape[1]
    tm = TOKEN_TILE
    const = dict(pipeline_mode=pl.Buffered(1))
    return pl.pallas_call(
        _inproj_kernel,
        grid=(bsz, seq // tm),
        in_specs=[pl.BlockSpec((1, tm, d), lambda b, t: (b, t, 0)),
                  pl.BlockSpec((N_MOD, bsz, d), lambda b, t: (0, 0, 0)),
                  pl.BlockSpec((1, d), lambda b, t: (0, 0)),
                  pl.BlockSpec((d, n), lambda b, t: (0, 0), **const),
                  pl.BlockSpec((tm, LANES), lambda b, t: (t, 0)),
                  pl.BlockSpec((tm, LANES), lambda b, t: (t, 0))],
        out_specs=pl.BlockSpec((1, tm, n), lambda b, t: (b, t, 0)),
        out_shape=jax.ShapeDtypeStruct((bsz, seq, n), jnp.bfloat16),
        scratch_shapes=[pltpu.VMEM((d, n), jnp.bfloat16)],
        compiler_params=pltpu.CompilerParams(dimension_semantics=("arbitrary", "arbitrary"),
                                             vmem_limit_bytes=VMEM_LIMIT),
        name="inproj",
    )(x, mod3, g.reshape(1, d), w, cos128, sin128)


def _attn_kernel(lq1_ref, lk1_ref, lq2_ref, lk2_ref, g_ref, dq_ref, dk_ref, dv_ref, sq_ref, sk_ref, sv_ref,
                 wo_ref, w1_ref, w2_ref, od_ref, os_ref, wo_bf_ref, w1_bf_ref, w2_bf_ref,
                 dvt_scr, dqm_scr, dacc_scr, m_scr, s_scr, p_scr, alpha_scr,
                 kp_scr, svt_scr, sqm_scr, sacc_scr, pre_scr, ap_scr, carry_scr, *, lambda_init):
    g = pl.program_id(1)
    nblk = dk_ref.shape[1]
    tb = ATT_BLOCK
    tw = 2 * tb
    dv = 2 * HEAD_DIM
    vrows = dv + ONES_ROWS
    nd = N_DIFF_HEADS
    ns = N_SB_HEADS
    d_last = Q_GROUP * nd - 1

    @pl.when(g == 0)
    def _():
        for blk in range(nblk):
            vt = dv_ref[0, blk].astype(jnp.float32).T.astype(jnp.bfloat16)
            for hd in range(nd):
                dvt_scr[blk, hd * vrows:hd * vrows + dv, :] = vt[hd * dv:(hd + 1) * dv, :]
                dvt_scr[blk, hd * vrows + dv:(hd + 1) * vrows, :] = jnp.ones((ONES_ROWS, tb), jnp.bfloat16)
        row = lax.broadcasted_iota(jnp.int32, (tb, tb), 0)
        col = lax.broadcasted_iota(jnp.int32, (tb, tb), 1)
        pos = SEG_LEN * (row % SUBLANES) + (SEG_LEN - 1 - row // SUBLANES)
        perm = jnp.where(col == pos, 1.0, 0.0).astype(jnp.bfloat16)
        for blk in range(nblk):
            kp = lax.dot_general(perm, sk_ref[0, blk], _NN, preferred_element_type=jnp.float32)
            kp_scr[blk] = kp.astype(jnp.bfloat16)
            vp = lax.dot_general(perm, sv_ref[0, blk], _NN, preferred_element_type=jnp.float32)
            svt_scr[blk] = vp.T.astype(jnp.bfloat16)

    wo_bf_ref[...] = wo_ref[...].astype(wo_bf_ref.dtype)
    w1_bf_ref[...] = w1_ref[...].astype(w1_bf_ref.dtype)
    w2_bf_ref[...] = w2_ref[...].astype(w2_bf_ref.dtype)

    lane = lax.broadcasted_iota(jnp.int32, (tb, LANES), 1)
    for qb in range(Q_GROUP):
        for c in range(nd):
            for ref, scr in ((dq_ref, dqm_scr), (sq_ref, sqm_scr)):
                qp = ref[0, qb, :, c * LANES:(c + 1) * LANES]
                zero = jnp.zeros_like(qp)
                scr[qb * nd + c, :tb, :] = jnp.where(lane < HEAD_DIM, qp, zero)
                scr[qb * nd + c, tb:, :] = jnp.where(lane >= HEAD_DIM, qp, zero)
        for h in range(ns):
            carry_scr[qb * ns + h] = jnp.ones((SUBLANES, tb), jnp.float32)
    dacc_scr[d_last] = jnp.zeros((vrows, tw), jnp.float32)
    alpha_scr[...] = jnp.zeros((1, tw), jnp.float32)

    sub = lax.broadcasted_iota(jnp.int32, (SUBLANES, tb), 0)
    qcol = lax.broadcasted_iota(jnp.int32, (SUBLANES, tb), 1)
    qcol2 = lax.broadcasted_iota(jnp.int32, (SUBLANES, tw), 1) % tb

    def d_scores(j, qb, hd):
        return lax.dot_general(dk_ref[0, j, :, hd * LANES:(hd + 1) * LANES], dqm_scr[qb * nd + hd], _NT,
                               preferred_element_type=jnp.float32)

    def d_elementwise(s_t, qb, hd, first):
        buf = hd % 2
        it = qb * nd + hd
        mx = None
        for v in range(tb // SUBLANES):
            sv = s_t[v * SUBLANES:(v + 1) * SUBLANES, :]
            if first:
                sv = jnp.where(qcol2 >= (v * SUBLANES // CHUNK) * CHUNK, sv, -jnp.inf)
            mx = sv if mx is None else jnp.maximum(mx, sv)
            s_scr[buf, v * SUBLANES:(v + 1) * SUBLANES, :] = sv
        tmax = jnp.max(mx, axis=0, keepdims=True)
        if first:
            m_new = tmax
            alpha = None
        else:
            m_old = m_scr[it]
            m_new = jnp.maximum(m_old, tmax)
            alpha = jnp.exp2(m_old - m_new)
        m_scr[it] = m_new
        return jnp.exp2(s_scr[buf] - m_new).astype(jnp.bfloat16), alpha

    def d_pv(j, hd, p):
        return lax.dot_general(dvt_scr[j, hd * vrows:(hd + 1) * vrows, :], p, _NN,
                               preferred_element_type=jnp.float32)

    def finish_deferred(j_prev):
        dacc_scr[d_last] = alpha_scr[...] * dacc_scr[d_last] + d_pv(j_prev, nd - 1, p_scr[...])

    def s_scores(j, qb, pair):
        return lax.dot_general(kp_scr[j, :, pair * LANES:(pair + 1) * LANES], sqm_scr[qb * N_PAIRS + pair], _NT,
                               preferred_element_type=jnp.float32)

    def s_elementwise(zh, qb, h, diag):
        buf = h % 2
        rem = jnp.ones((SUBLANES, tb), jnp.float32)
        prev = None
        for v in range(SEG_LEN):
            beta = 0.5 * jnp.tanh(zh[v * SUBLANES:(v + 1) * SUBLANES, :]) + 0.5
            if diag:
                beta = jnp.where(SEG_LEN * sub + (SEG_LEN - 1 - v) < qcol, beta, 0.0)
            a_v = beta * rem
            rem = rem - a_v
            if v % 2 == 0:
                prev = a_v
            else:
                ap_scr[buf, (v - 1) * SUBLANES:(v + 1) * SUBLANES, :] = jnp.concatenate(
                    [prev, a_v], axis=0).astype(jnp.bfloat16)
        incl = rem
        for k in (1, 2, 4):
            shifted = pltpu.roll(incl, SUBLANES - k, 0)
            incl = incl * jnp.where(sub + k < SUBLANES, shifted, 1.0)
        excl = jnp.where(sub + 1 < SUBLANES, pltpu.roll(incl, SUBLANES - 1, 0), 1.0)
        carry = carry_scr[qb * ns + h]
        start = carry * excl
        carry_scr[qb * ns + h] = carry * jnp.broadcast_to(incl[0:1, :], carry.shape)
        start2 = jnp.concatenate([start, start], axis=0).astype(jnp.bfloat16)
        return jnp.concatenate(
            [ap_scr[buf, u * 2 * SUBLANES:(u + 1) * 2 * SUBLANES, :] * start2 for u in range(SEG_LEN // 2)],
            axis=0)

    def s_av(j, h, a):
        return lax.dot_general(svt_scr[j, h * HEAD_DIM:(h + 1) * HEAD_DIM, :], a, _NN,
                               preferred_element_type=jnp.float32)

    def tile_body(j, blocks, nxt, j_prev):
        if j_prev is not None:
            finish_deferred(j_prev)
        entries = []
        for qb, masked in blocks:
            for c in range(nd):
                entries.append(("s", qb, c, masked))
                entries.append(("d", qb, c, masked))
        sc = [None] * len(entries)
        sc[0] = pre_scr[...]
        issued = 1

        def issue(upto):
            nonlocal issued
            while issued <= upto:
                if issued < len(entries):
                    kind, qb, c, _ = entries[issued]
                    sc[issued] = s_scores(j, qb, c) if kind == "s" else d_scores(j, qb, c)
                elif issued == len(entries):
                    pre_scr[...] = s_scores(nxt[0], nxt[1], 0)
                issued += 1

        for k, (kind, qb, c, masked) in enumerate(entries):
            issue(k + 2)
            if kind == "s":
                for hh in range(2):
                    h = 2 * c + hh
                    a = s_elementwise(sc[k][:, hh * tb:(hh + 1) * tb], qb, h, masked)
                    rows = slice(h * HEAD_DIM, (h + 1) * HEAD_DIM)
                    if masked:
                        sacc_scr[qb, rows, :] = s_av(j, h, a)
                    else:
                        sacc_scr[qb, rows, :] += s_av(j, h, a)
            else:
                p, alpha = d_elementwise(sc[k], qb, c, masked)
                it = qb * nd + c
                if it == d_last:
                    p_scr[...] = p
                    if not masked:
                        alpha_scr[...] = alpha
                elif masked:
                    dacc_scr[it] = d_pv(j, c, p)
                else:
                    dacc_scr[it] = alpha * dacc_scr[it] + d_pv(j, c, p)
            sc[k] = None

    base = g * Q_GROUP
    pre_scr[...] = s_scores(base + Q_GROUP - 1, Q_GROUP - 1, 0)
    j_prev = None
    for d in range(Q_GROUP - 1, -1, -1):
        blocks = [(d, True)] + [(qb, False) for qb in range(d + 1, Q_GROUP)]
        nxt = (base + d - 1, d - 1) if d > 0 else (jnp.maximum(base - 1, 0), 0)
        tile_body(base + d, blocks, nxt, j_prev)
        j_prev = base + d

    def body(t, c):
        j = base - 1 - t
        tile_body(j, [(qb, False) for qb in range(Q_GROUP)], (jnp.maximum(j - 1, 0), 0), j + 1)
        return c

    if nblk > Q_GROUP:
        lax.fori_loop(0, base, body, 0)
    finish_deferred(0)

    g_col = jnp.broadcast_to(g_ref[...], (dv, dv)).T
    g_col = jnp.concatenate([g_col] * (tb // dv), axis=1)
    lam = (jnp.exp(jnp.sum(lq1_ref[...] * lk1_ref[...], axis=-1, keepdims=True))
           - jnp.exp(jnp.sum(lq2_ref[...] * lk2_ref[...], axis=-1, keepdims=True)) + lambda_init)
    for qb in range(Q_GROUP):
        for hd in range(nd):
            a1 = dacc_scr[qb * nd + hd, :, :tb]
            a2 = dacc_scr[qb * nd + hd, :, tb:]
            o_t = a1[:dv] / a1[dv:dv + 1] - lam * (a2[:dv] / a2[dv:dv + 1])
            ms = jnp.mean(o_t * o_t, axis=0, keepdims=True)
            o_t = o_t * lax.rsqrt(ms + NORM_EPS) * g_col * (1.0 - lambda_init)
            od_ref[0, qb * tb:(qb + 1) * tb, hd * LANES:(hd + 1) * LANES] = o_t.T.astype(od_ref.dtype)
        os_ref[0, qb * tb:(qb + 1) * tb, :] = sacc_scr[qb].T.astype(os_ref.dtype)


def _attn_call(proj4, lq1, lk1, lq2, lk2, subln_g, lambda_init, w_out, w_ff1, w_ff2):
    bsz, nblk, tb, _ = proj4.shape
    seq = nblk * tb
    w = DIFF_WIDTH
    assert SB_WIDTH == w and nblk % Q_GROUP == 0
    gsteps = nblk // Q_GROUP
    nsteps = bsz * gsteps

    def slab(arr):
        rows = arr.shape[0] // nsteps
        assert rows * nsteps == arr.shape[0] and rows % (2 * SUBLANES) == 0
        return pl.BlockSpec((rows, arr.shape[1]), lambda b, g: (b * gsteps + g, 0))

    wspecs = [slab(w_out), slab(w_ff1), slab(w_ff2)]
    wshapes = [jax.ShapeDtypeStruct(a.shape, jnp.bfloat16) for a in (w_out, w_ff1, w_ff2)]
    vrows = 2 * HEAD_DIM + ONES_ROWS
    nit = Q_GROUP * N_DIFF_HEADS
    vec = lambda a: a.reshape(1, HEAD_DIM)
    small = pl.BlockSpec((1, HEAD_DIM), lambda b, g: (0, 0))
    qspec = lambda col: pl.BlockSpec((1, Q_GROUP, tb, w), lambda b, g: (b, g, 0, col))
    kvspec = lambda col: pl.BlockSpec((1, nblk, tb, w), lambda b, g: (b, 0, 0, col))
    ospec = pl.BlockSpec((1, Q_GROUP * tb, w), lambda b, g: (b, g, 0))
    oshape = jax.ShapeDtypeStruct((bsz, seq, w), jnp.bfloat16)
    return pl.pallas_call(
        functools.partial(_attn_kernel, lambda_init=lambda_init),
        grid=(bsz, nblk // Q_GROUP),
        in_specs=[small, small, small, small,
                  pl.BlockSpec((1, 2 * HEAD_DIM), lambda b, g: (0, 0)),
                  qspec(0), kvspec(1), kvspec(2), qspec(3), kvspec(4), kvspec(5)] + wspecs,
        out_specs=[ospec, ospec] + wspecs,
        out_shape=[oshape, oshape] + wshapes,
        scratch_shapes=[pltpu.VMEM((nblk, N_DIFF_HEADS * vrows, tb), jnp.bfloat16),
                        pltpu.VMEM((nit, 2 * tb, LANES), jnp.bfloat16),
                        pltpu.VMEM((nit, vrows, 2 * tb), jnp.float32),
                        pltpu.VMEM((nit, 1, 2 * tb), jnp.float32),
                        pltpu.VMEM((2, tb, 2 * tb), jnp.float32),
                        pltpu.VMEM((tb, 2 * tb), jnp.bfloat16),
                        pltpu.VMEM((1, 2 * tb), jnp.float32),
                        pltpu.VMEM((nblk, tb, w), jnp.bfloat16),
                        pltpu.VMEM((nblk, w, tb), jnp.bfloat16),
                        pltpu.VMEM((Q_GROUP * N_PAIRS, 2 * tb, LANES), jnp.bfloat16),
                        pltpu.VMEM((Q_GROUP, w, tb), jnp.float32),
                        pltpu.VMEM((tb, 2 * tb), jnp.float32),
                        pltpu.VMEM((2, tb, tb), jnp.bfloat16),
                        pltpu.VMEM((Q_GROUP * N_SB_HEADS, SUBLANES, tb), jnp.float32)],
        compiler_params=pltpu.CompilerParams(
            dimension_semantics=("arbitrary", "arbitrary"), vmem_limit_bytes=VMEM_LIMIT),
        name="attn",
    )(vec(lq1), vec(lk1), vec(lq2), vec(lk2), subln_g.reshape(1, 2 * HEAD_DIM),
      proj4, proj4, proj4, proj4, proj4, proj4, w_out, w_ff1, w_ff2)


def _out_ffn_kernel(x_ref, od_ref, os_ref, mod_ref, wo_ref, gf_ref, w1_ref, w2_ref, gl_ref, o_ref):
    b = pl.program_id(0)
    g_m = mod_ref[2, pl.ds(b, 1), :]
    sh_f = mod_ref[3, pl.ds(b, 1), :]
    sc_f = mod_ref[4, pl.ds(b, 1), :]
    g_f = mod_ref[5, pl.ds(b, 1), :]
    d_ff = w1_ref.shape[1]
    fc = 1024
    tm = x_ref.shape[1]
    halves = [slice(k * (tm // 2), (k + 1) * (tm // 2)) for k in range(2)]
    x1, h, f = [None, None], [None, None], [None, None]
    for k, rows in enumerate(halves):
        mixed = (lax.dot_general(od_ref[0, rows, :], wo_ref[:DIFF_WIDTH, :], _NN, preferred_element_type=jnp.float32)
                 + lax.dot_general(os_ref[0, rows, :], wo_ref[DIFF_WIDTH:, :], _NN,
                                   preferred_element_type=jnp.float32))
        x1[k] = x_ref[0, rows, :] + g_m * mixed
    for k in range(2):
        h[k] = ((x1[k] * _rms_scale(x1[k]) * gf_ref[...]) * (1.0 + sc_f) + sh_f).astype(jnp.bfloat16)
    for c in range(d_ff // fc):
        for k in range(2):
            u = lax.dot_general(h[k], w1_ref[:, c * fc:(c + 1) * fc], _NN, preferred_element_type=jnp.float32)
            r = jnp.square(jnp.maximum(u, 0.0)).astype(jnp.bfloat16)
            part = lax.dot_general(r, w2_ref[c * fc:(c + 1) * fc, :], _NN, preferred_element_type=jnp.float32)
            f[k] = part if f[k] is None else f[k] + part
    for k, rows in enumerate(halves):
        x2 = x1[k] + g_f * f[k]
        o_ref[0, rows, :] = x2 * _rms_scale(x2) * gl_ref[...]


def _out_ffn_call(x, o_diff, o_sb, mod3, wo, gf, w1, w2, gl):
    bsz, seq, d = x.shape
    tm = TOKEN_TILE
    const = dict(pipeline_mode=pl.Buffered(1))
    tok = lambda w: pl.BlockSpec((1, tm, w), lambda b, t: (b, t, 0))
    return pl.pallas_call(
        _out_ffn_kernel,
        grid=(bsz, seq // tm),
        in_specs=[tok(d), tok(DIFF_WIDTH), tok(SB_WIDTH),
                  pl.BlockSpec((N_MOD, bsz, d), lambda b, t: (0, 0, 0)),
                  pl.BlockSpec(wo.shape, lambda b, t: (0, 0), **const),
                  pl.BlockSpec((1, d), lambda b, t: (0, 0)),
                  pl.BlockSpec(w1.shape, lambda b, t: (0, 0), **const),
                  pl.BlockSpec(w2.shape, lambda b, t: (0, 0), **const),
                  pl.BlockSpec((1, d), lambda b, t: (0, 0))],
        out_specs=tok(d),
        out_shape=jax.ShapeDtypeStruct((bsz, seq, d), jnp.float32),
        compiler_params=pltpu.CompilerParams(dimension_semantics=("arbitrary", "arbitrary"),
                                             vmem_limit_bytes=VMEM_LIMIT),
        name="out_ffn",
    )(x, o_diff, o_sb, mod3, wo, gf.reshape(1, d), w1, w2, gl.reshape(1, d))


def _rope_tables(seq_len):
    dim = HEAD_DIM
    inv = (1.0 / (np.float32(ROPE_THETA) ** (np.arange(0, dim, 2, dtype=np.float32) / np.float32(dim)))).astype(np.float32)
    ang = np.arange(seq_len, dtype=np.float32)[:, None] * inv[None, :]
    ang = np.concatenate([ang, ang], axis=-1)
    cos, sin = np.cos(ang).astype(np.float32), np.sin(ang).astype(np.float32)
    sign = np.where(np.arange(dim) < dim // 2, -1.0, 1.0).astype(np.float32)
    reps = LANES // dim
    return jnp.asarray(np.tile(cos, (1, reps))), jnp.asarray(np.tile(sin * sign[None, :], (1, reps)))


def kernel(x, c, ada_w, ada_b, mix_norm_g, w_in, lambda_q1, lambda_k1, lambda_q2, lambda_k2,
           diff_subln_g, w_out, ffn_norm_g, w_ff1, w_ff2, final_norm_g):
    bsz, seq, d = x.shape
    depth = ada_w.shape[0]
    assert depth == 1, "the fused final norm assumes a single layer"
    assert seq % TOKEN_TILE == 0 and seq % ATT_BLOCK == 0
    cos128, sin128 = _rope_tables(seq)
    layer = 0
    lambda_init = 0.8 - 0.6 * math.exp(-0.3 * layer)
    mod3 = _adaln_call(c, ada_w[layer], ada_b[layer])
    proj = _inproj_call(x, mod3, mix_norm_g[layer], w_in[layer], cos128, sin128)
    proj4 = proj.reshape(bsz, seq // ATT_BLOCK, ATT_BLOCK, proj.shape[-1])
    o_diff, o_sb, wo_bf, w1_bf, w2_bf = _attn_call(
        proj4, lambda_q1[layer], lambda_k1[layer], lambda_q2[layer], lambda_k2[layer], diff_subln_g[layer],
        lambda_init, w_out[layer], w_ff1[layer], w_ff2[layer])
    return _out_ffn_call(x, o_diff, o_sb, mod3, wo_bf, ffn_norm_g[layer], w1_bf, w2_bf, final_norm_g)
```

```python
import functools
import math

import jax
import jax.numpy as jnp
import numpy as np
from jax import lax
from jax.experimental import pallas as pl
from jax.experimental.pallas import tpu as pltpu

HEAD_DIM = 64
N_DIFF_HEADS = 4
N_SB_HEADS = 8
DIFF_WIDTH = N_DIFF_HEADS * 2 * HEAD_DIM
SB_WIDTH = N_SB_HEADS * HEAD_DIM
CHUNK = 64
ROPE_THETA = 10000.0
NORM_EPS = 1e-6
N_MOD = 6

LANES = 128
SUBLANES = 8
ATT_BLOCK = 256
SEG_LEN = ATT_BLOCK // SUBLANES
TOKEN_TILE = 512
N_PAIRS = N_SB_HEADS // 2
DIFF_Q_GROUP = 4
SB_Q_GROUP = 8
ONES_ROWS = 16
LOG2E = 1.4426950408889634
VMEM_LIMIT = 56 * 1024 * 1024

_NT = (((1,), (1,)), ((), ()))
_NN = (((1,), (0,)), ((), ()))


def _rms_scale(xf):
    return lax.rsqrt(jnp.mean(xf * xf, axis=-1, keepdims=True) + NORM_EPS)


def _adaln_kernel(c_ref, w_ref, b_ref, o_ref):
    c = c_ref[...]
    ca = c / (1.0 + jnp.exp(-c))
    o_ref[0] = lax.dot_general(ca.astype(jnp.bfloat16), w_ref[...].astype(jnp.bfloat16), _NN,
                               preferred_element_type=jnp.float32) + b_ref[...]


def _adaln_call(c, w, b):
    bsz, d = c.shape
    n = w.shape[1]
    tn = d
    return pl.pallas_call(
        _adaln_kernel,
        grid=(n // tn,),
        in_specs=[pl.BlockSpec((bsz, d), lambda j: (0, 0)),
                  pl.BlockSpec((d, tn), lambda j: (0, j)),
                  pl.BlockSpec((1, tn), lambda j: (0, j))],
        out_specs=pl.BlockSpec((1, bsz, tn), lambda j: (j, 0, 0)),
        out_shape=jax.ShapeDtypeStruct((n // tn, bsz, tn), jnp.float32),
        compiler_params=pltpu.CompilerParams(dimension_semantics=("arbitrary",),
                                             vmem_limit_bytes=VMEM_LIMIT),
        name="adaln",
    )(c, w, b.reshape(1, n))


def _inproj_kernel(x_ref, mod_ref, g_ref, w_ref, cos_ref, sin_ref, o_ref, wb_scr):
    b = pl.program_id(0)
    gw = DIFF_WIDTH

    @pl.when((b == 0) & (pl.program_id(1) == 0))
    def _():
        for n in range(w_ref.shape[1] // gw):
            wb_scr[:, n * gw:(n + 1) * gw] = w_ref[:, n * gw:(n + 1) * gw].astype(wb_scr.dtype)

    xf = x_ref[0]
    sh = mod_ref[0, pl.ds(b, 1), :]
    sc = mod_ref[1, pl.ds(b, 1), :]
    h = (xf * _rms_scale(xf) * g_ref[...]) * (1.0 + sc) + sh
    hb = h.astype(jnp.bfloat16)
    cos = cos_ref[...]
    sin = sin_ref[...]
    lane = lax.broadcasted_iota(jnp.int32, cos.shape, 1)
    first_half = (lane % HEAD_DIM) < (HEAD_DIM // 2)
    scale = HEAD_DIM ** -0.5
    for n in range(6):
        acc = lax.dot_general(hb, wb_scr[:, n * gw:(n + 1) * gw], _NN,
                              preferred_element_type=jnp.float32)
        if n < 2:
            for j in range(gw // LANES):
                xb = acc[:, j * LANES:(j + 1) * LANES]
                rot = jnp.where(first_half, pltpu.roll(xb, LANES - HEAD_DIM // 2, 1),
                                pltpu.roll(xb, HEAD_DIM // 2, 1))
                yb = xb * cos + rot * sin
                if n == 0:
                    yb = yb * (scale * LOG2E)
                o_ref[0, :, n * gw + j * LANES:n * gw + (j + 1) * LANES] = yb.astype(o_ref.dtype)
        else:
            if n == 3:
                acc = acc * (0.5 * scale)
            o_ref[0, :, n * gw:(n + 1) * gw] = acc.astype(o_ref.dtype)


def _inproj_call(x, mod3, g, w, cos128, sin128):
    bsz, seq, d = x.shape
    n = w.shape[1]
    tm = TOKEN_TILE
    const = dict(pipeline_mode=pl.Buffered(1))
    return pl.pallas_call(
        _inproj_kernel,
        grid=(bsz, seq // tm),
        in_specs=[pl.BlockSpec((1, tm, d), lambda b, t: (b, t, 0)),
                  pl.BlockSpec((N_MOD, bsz, d), lambda b, t: (0, 0, 0)),
                  pl.BlockSpec((1, d), lambda b, t: (0, 0)),
                  pl.BlockSpec((d, n), lambda b, t: (0, 0), **const),
                  pl.BlockSpec((tm, LANES), lambda b, t: (t, 0)),
                  pl.BlockSpec((tm, LANES), lambda b, t: (t, 0))],
        out_specs=pl.BlockSpec((1, tm, n), lambda b, t: (b, t, 0)),
        out_shape=jax.ShapeDtypeStruct((bsz, seq, n), jnp.bfloat16),
        scratch_shapes=[pltpu.VMEM((d, n), jnp.bfloat16)],
        compiler_params=pltpu.CompilerParams(dimension_semantics=("arbitrary", "arbitrary"),
                                             vmem_limit_bytes=VMEM_LIMIT),
        name="inproj",
    )(x, mod3, g.reshape(1, d), w, cos128, sin128)


def _diff_kernel(lq1_ref, lk1_ref, lq2_ref, lk2_ref, g_ref, q_ref, k_ref, v_ref, wo_ref, w1_ref, w2_ref,
                 o_ref, wo_bf_ref, w1_bf_ref, w2_bf_ref,
                 vt_scr, qm_scr, acc_scr, m_scr, pre_scr, s_scr, p_scr, alpha_scr, *, lambda_init):
    g = pl.program_id(1)
    nblk = v_ref.shape[1]
    tb = ATT_BLOCK
    tw = 2 * tb
    dv = 2 * HEAD_DIM
    vrows = dv + ONES_ROWS
    nh = N_DIFF_HEADS
    last_item = (DIFF_Q_GROUP - 1, nh - 1)

    @pl.when(g == 0)
    def _():
        for blk in range(nblk):
            vt = v_ref[0, blk].astype(jnp.float32).T.astype(jnp.bfloat16)
            for hd in range(nh):
                vt_scr[blk, hd * vrows:hd * vrows + dv, :] = vt[hd * dv:(hd + 1) * dv, :]
                vt_scr[blk, hd * vrows + dv:(hd + 1) * vrows, :] = jnp.ones((ONES_ROWS, tb), jnp.bfloat16)

    wo_bf_ref[...] = wo_ref[...].astype(wo_bf_ref.dtype)
    w1_bf_ref[...] = w1_ref[...].astype(w1_bf_ref.dtype)
    w2_bf_ref[...] = w2_ref[...].astype(w2_bf_ref.dtype)

    lane = lax.broadcasted_iota(jnp.int32, (tb, LANES), 1)
    for qb in range(DIFF_Q_GROUP):
        for hd in range(nh):
            qp = q_ref[0, qb, :, hd * LANES:(hd + 1) * LANES]
            zero = jnp.zeros_like(qp)
            qm_scr[qb * nh + hd, :tb, :] = jnp.where(lane < HEAD_DIM, qp, zero)
            qm_scr[qb * nh + hd, tb:, :] = jnp.where(lane >= HEAD_DIM, qp, zero)
    acc_scr[DIFF_Q_GROUP * nh - 1] = jnp.zeros((vrows, tw), jnp.float32)
    alpha_scr[...] = jnp.zeros((1, tw), jnp.float32)

    qcol = lax.broadcasted_iota(jnp.int32, (SUBLANES, tw), 1) % tb

    def scores(j, qb, hd):
        return lax.dot_general(k_ref[0, j, :, hd * LANES:(hd + 1) * LANES], qm_scr[qb * nh + hd], _NT,
                               preferred_element_type=jnp.float32)

    def elementwise(s_t, qb, hd, first):
        buf = hd % 2
        it = qb * nh + hd
        mx = None
        for v in range(tb // SUBLANES):
            sv = s_t[v * SUBLANES:(v + 1) * SUBLANES, :]
            if first:
                sv = jnp.where(qcol >= (v * SUBLANES // CHUNK) * CHUNK, sv, -jnp.inf)
            mx = sv if mx is None else jnp.maximum(mx, sv)
            s_scr[buf, v * SUBLANES:(v + 1) * SUBLANES, :] = sv
        tmax = jnp.max(mx, axis=0, keepdims=True)
        if first:
            m_new = tmax
            alpha = None
        else:
            m_old = m_scr[it]
            m_new = jnp.maximum(m_old, tmax)
            alpha = jnp.exp2(m_old - m_new)
        m_scr[it] = m_new
        return jnp.exp2(s_scr[buf] - m_new).astype(jnp.bfloat16), alpha

    def pv_product(j, hd, p):
        return lax.dot_general(vt_scr[j, hd * vrows:(hd + 1) * vrows, :], p, _NN,
                               preferred_element_type=jnp.float32)

    def finish_deferred(j_prev):
        it = DIFF_Q_GROUP * nh - 1
        acc_scr[it] = alpha_scr[...] * acc_scr[it] + pv_product(j_prev, nh - 1, p_scr[...])

    def tile_body(j, blocks, nxt, j_prev):
        if j_prev is not None:
            finish_deferred(j_prev)
        items = [(qb, hd, masked) for qb, masked in blocks for hd in range(nh)]
        ss = [None] * len(items)
        ss[0] = pre_scr[...]
        issued = 1

        def issue(upto):
            nonlocal issued
            while issued <= upto:
                if issued < len(items):
                    ss[issued] = scores(j, items[issued][0], items[issued][1])
                elif issued == len(items):
                    pre_scr[...] = scores(nxt[0], nxt[1], 0)
                issued += 1

        for k, (qb, hd, masked) in enumerate(items):
            issue(k + 2)
            p, alpha = elementwise(ss[k], qb, hd, masked)
            ss[k] = None
            it = qb * nh + hd
            if (qb, hd) == last_item:
                p_scr[...] = p
                if not masked:
                    alpha_scr[...] = alpha
            elif masked:
                acc_scr[it] = pv_product(j, hd, p)
            else:
                acc_scr[it] = alpha * acc_scr[it] + pv_product(j, hd, p)

    base = g * DIFF_Q_GROUP
    pre_scr[...] = scores(base + DIFF_Q_GROUP - 1, DIFF_Q_GROUP - 1, 0)
    j_prev = None
    for d in range(DIFF_Q_GROUP - 1, -1, -1):
        blocks = [(d, True)] + [(qb, False) for qb in range(d + 1, DIFF_Q_GROUP)]
        nxt = (base + d - 1, d - 1) if d > 0 else (jnp.maximum(base - 1, 0), 0)
        tile_body(base + d, blocks, nxt, j_prev)
        j_prev = base + d

    def body(t, c):
        j = base - 1 - t
        tile_body(j, [(qb, False) for qb in range(DIFF_Q_GROUP)], (jnp.maximum(j - 1, 0), 0), j + 1)
        return c

    if nblk > DIFF_Q_GROUP:
        lax.fori_loop(0, base, body, 0)
    finish_deferred(0)

    g_col = jnp.broadcast_to(g_ref[...], (dv, dv)).T
    g_col = jnp.concatenate([g_col] * (tb // dv), axis=1)
    lam = (jnp.exp(jnp.sum(lq1_ref[...] * lk1_ref[...], axis=-1, keepdims=True))
           - jnp.exp(jnp.sum(lq2_ref[...] * lk2_ref[...], axis=-1, keepdims=True)) + lambda_init)
    for qb in range(DIFF_Q_GROUP):
        for hd in range(nh):
            a1 = acc_scr[qb * nh + hd, :, :tb]
            a2 = acc_scr[qb * nh + hd, :, tb:]
            o_t = a1[:dv] / a1[dv:dv + 1] - lam * (a2[:dv] / a2[dv:dv + 1])
            ms = jnp.mean(o_t * o_t, axis=0, keepdims=True)
            o_t = o_t * lax.rsqrt(ms + NORM_EPS) * g_col * (1.0 - lambda_init)
            o_ref[0, qb * tb:(qb + 1) * tb, hd * LANES:(hd + 1) * LANES] = o_t.T.astype(o_ref.dtype)


def _diff_attn_call(proj4, lq1, lk1, lq2, lk2, subln_g, lambda_init, w_out, w_ff1, w_ff2):
    bsz, nblk, tb, _ = proj4.shape
    seq = nblk * tb
    w = DIFF_WIDTH
    gsteps = nblk // DIFF_Q_GROUP
    nsteps = bsz * gsteps

    def slab(arr):
        rows = arr.shape[0] // nsteps
        assert rows * nsteps == arr.shape[0] and rows % (2 * SUBLANES) == 0
        return pl.BlockSpec((rows, arr.shape[1]), lambda b, g: (b * gsteps + g, 0))

    wspecs = [slab(w_out), slab(w_ff1), slab(w_ff2)]
    wshapes = [jax.ShapeDtypeStruct(a.shape, jnp.bfloat16) for a in (w_out, w_ff1, w_ff2)]
    vrows = 2 * HEAD_DIM + ONES_ROWS
    nit = DIFF_Q_GROUP * N_DIFF_HEADS
    assert nblk % DIFF_Q_GROUP == 0
    vec = lambda a: a.reshape(1, HEAD_DIM)
    small = pl.BlockSpec((1, HEAD_DIM), lambda b, g: (0, 0))
    return pl.pallas_call(
        functools.partial(_diff_kernel, lambda_init=lambda_init),
        grid=(bsz, nblk // DIFF_Q_GROUP),
        in_specs=[small, small, small, small,
                  pl.BlockSpec((1, 2 * HEAD_DIM), lambda b, g: (0, 0)),
                  pl.BlockSpec((1, DIFF_Q_GROUP, tb, w), lambda b, g: (b, g, 0, 0)),
                  pl.BlockSpec((1, nblk, tb, w), lambda b, g: (b, 0, 0, 1)),
                  pl.BlockSpec((1, nblk, tb, w), lambda b, g: (b, 0, 0, 2))] + wspecs,
        out_specs=[pl.BlockSpec((1, DIFF_Q_GROUP * tb, w), lambda b, g: (b, g, 0))] + wspecs,
        out_shape=[jax.ShapeDtypeStruct((bsz, seq, w), jnp.bfloat16)] + wshapes,
        scratch_shapes=[pltpu.VMEM((nblk, N_DIFF_HEADS * vrows, tb), jnp.bfloat16),
                        pltpu.VMEM((nit, 2 * tb, LANES), jnp.bfloat16),
                        pltpu.VMEM((nit, vrows, 2 * tb), jnp.float32),
                        pltpu.VMEM((nit, 1, 2 * tb), jnp.float32),
                        pltpu.VMEM((tb, 2 * tb), jnp.float32),
                        pltpu.VMEM((2, tb, 2 * tb), jnp.float32),
                        pltpu.VMEM((tb, 2 * tb), jnp.bfloat16),
                        pltpu.VMEM((1, 2 * tb), jnp.float32)],
        compiler_params=pltpu.CompilerParams(
            dimension_semantics=("arbitrary", "arbitrary"), vmem_limit_bytes=VMEM_LIMIT),
        name="diff_attn",
    )(vec(lq1), vec(lk1), vec(lq2), vec(lk2), subln_g.reshape(1, 2 * HEAD_DIM), proj4, proj4, proj4,
      w_out, w_ff1, w_ff2)


def _sb_kernel(q_ref, k_ref, v_ref, o_ref, kp_scr, vt_scr, qm_scr, acc_scr, pre_scr, ap_scr, carry_scr, a_scr):
    g = pl.program_id(1)
    nblk = k_ref.shape[1]
    tb = ATT_BLOCK
    nh = N_SB_HEADS
    last_item = (SB_Q_GROUP - 1, nh - 1)

    @pl.when(g == 0)
    def _():
        row = lax.broadcasted_iota(jnp.int32, (tb, tb), 0)
        col = lax.broadcasted_iota(jnp.int32, (tb, tb), 1)
        pos = SEG_LEN * (row % SUBLANES) + (SEG_LEN - 1 - row // SUBLANES)
        perm = jnp.where(col == pos, 1.0, 0.0).astype(jnp.bfloat16)
        for blk in range(nblk):
            kp = lax.dot_general(perm, k_ref[0, blk], _NN, preferred_element_type=jnp.float32)
            kp_scr[blk] = kp.astype(jnp.bfloat16)
            vp = lax.dot_general(perm, v_ref[0, blk], _NN, preferred_element_type=jnp.float32)
            vt_scr[blk] = vp.T.astype(jnp.bfloat16)

    lane = lax.broadcasted_iota(jnp.int32, (tb, LANES), 1)
    for qb in range(SB_Q_GROUP):
        for pr in range(N_PAIRS):
            qp = q_ref[0, qb, :, pr * LANES:(pr + 1) * LANES]
            zero = jnp.zeros_like(qp)
            qm_scr[qb * N_PAIRS + pr, :tb, :] = jnp.where(lane < HEAD_DIM, qp, zero)
            qm_scr[qb * N_PAIRS + pr, tb:, :] = jnp.where(lane >= HEAD_DIM, qp, zero)
        for h in range(nh):
            carry_scr[qb * nh + h] = jnp.ones((SUBLANES, tb), jnp.float32)
    acc_scr[last_item[0], last_item[1] * HEAD_DIM:, :] = jnp.zeros((HEAD_DIM, tb), jnp.float32)

    sub = lax.broadcasted_iota(jnp.int32, (SUBLANES, tb), 0)
    qcol = lax.broadcasted_iota(jnp.int32, (SUBLANES, tb), 1)

    def scores(j, qb, pair):
        return lax.dot_general(kp_scr[j, :, pair * LANES:(pair + 1) * LANES], qm_scr[qb * N_PAIRS + pair], _NT,
                               preferred_element_type=jnp.float32)

    def elementwise(zh, qb, h, diag):
        buf = h % 2
        rem = jnp.ones((SUBLANES, tb), jnp.float32)
        prev = None
        for v in range(SEG_LEN):
            beta = 0.5 * jnp.tanh(zh[v * SUBLANES:(v + 1) * SUBLANES, :]) + 0.5
            if diag:
                beta = jnp.where(SEG_LEN * sub + (SEG_LEN - 1 - v) < qcol, beta, 0.0)
            a_v = beta * rem
            rem = rem - a_v
            if v % 2 == 0:
                prev = a_v
            else:
                ap_scr[buf, (v - 1) * SUBLANES:(v + 1) * SUBLANES, :] = jnp.concatenate(
                    [prev, a_v], axis=0).astype(jnp.bfloat16)
        incl = rem
        for k in (1, 2, 4):
            shifted = pltpu.roll(incl, SUBLANES - k, 0)
            incl = incl * jnp.where(sub + k < SUBLANES, shifted, 1.0)
        excl = jnp.where(sub + 1 < SUBLANES, pltpu.roll(incl, SUBLANES - 1, 0), 1.0)
        carry = carry_scr[qb * nh + h]
        start = carry * excl
        carry_scr[qb * nh + h] = carry * jnp.broadcast_to(incl[0:1, :], carry.shape)
        start2 = jnp.concatenate([start, start], axis=0).astype(jnp.bfloat16)
        return jnp.concatenate(
            [ap_scr[buf, u * 2 * SUBLANES:(u + 1) * 2 * SUBLANES, :] * start2 for u in range(SEG_LEN // 2)],
            axis=0)

    def av_product(j, h, a):
        return lax.dot_general(vt_scr[j, h * HEAD_DIM:(h + 1) * HEAD_DIM, :], a, _NN,
                               preferred_element_type=jnp.float32)

    def finish_deferred(j_prev):
        qb, h = last_item
        acc_scr[qb, h * HEAD_DIM:, :] += av_product(j_prev, h, a_scr[...])

    def tile_body(j, blocks, nxt, j_prev):
        if j_prev is not None:
            finish_deferred(j_prev)
        pairs = [(qb, pr, masked) for qb, masked in blocks for pr in range(N_PAIRS)]
        zs = [None] * len(pairs)
        zs[0] = pre_scr[...]
        issued = 1

        def issue(upto):
            nonlocal issued
            while issued <= upto:
                if issued < len(pairs):
                    zs[issued] = scores(j, pairs[issued][0], pairs[issued][1])
                elif issued == len(pairs):
                    pre_scr[...] = scores(nxt[0], nxt[1], 0)
                issued += 1

        for k, (qb, pr, masked) in enumerate(pairs):
            issue(k + 2)
            for hh in range(2):
                h = 2 * pr + hh
                a = elementwise(zs[k][:, hh * tb:(hh + 1) * tb], qb, h, masked)
                rows = slice(h * HEAD_DIM, (h + 1) * HEAD_DIM)
                first_write = masked
                if (qb, h) == last_item:
                    a_scr[...] = a
                elif first_write:
                    acc_scr[qb, rows, :] = av_product(j, h, a)
                else:
                    acc_scr[qb, rows, :] += av_product(j, h, a)
            zs[k] = None

    base = g * SB_Q_GROUP
    pre_scr[...] = scores(base + SB_Q_GROUP - 1, SB_Q_GROUP - 1, 0)
    j_prev = None
    for d in range(SB_Q_GROUP - 1, -1, -1):
        blocks = [(d, True)] + [(qb, False) for qb in range(d + 1, SB_Q_GROUP)]
        nxt = (base + d - 1, d - 1) if d > 0 else (jnp.maximum(base - 1, 0), 0)
        tile_body(base + d, blocks, nxt, j_prev)
        j_prev = base + d

    def body(t, c):
        j = base - 1 - t
        tile_body(j, [(qb, False) for qb in range(SB_Q_GROUP)], (jnp.maximum(j - 1, 0), 0), j + 1)
        return c

    if nblk > SB_Q_GROUP:
        lax.fori_loop(0, base, body, 0)
    finish_deferred(0)
    for qb in range(SB_Q_GROUP):
        o_ref[0, qb * tb:(qb + 1) * tb, :] = acc_scr[qb].T.astype(o_ref.dtype)


def _sb_attn_call(proj4):
    bsz, nblk, tb, _ = proj4.shape
    seq = nblk * tb
    w = SB_WIDTH
    base = 3 * DIFF_WIDTH // w
    assert nblk % SB_Q_GROUP == 0
    return pl.pallas_call(
        _sb_kernel,
        grid=(bsz, nblk // SB_Q_GROUP),
        in_specs=[pl.BlockSpec((1, SB_Q_GROUP, tb, w), lambda b, g: (b, g, 0, base)),
                  pl.BlockSpec((1, nblk, tb, w), lambda b, g: (b, 0, 0, base + 1)),
                  pl.BlockSpec((1, nblk, tb, w), lambda b, g: (b, 0, 0, base + 2))],
        out_specs=pl.BlockSpec((1, SB_Q_GROUP * tb, w), lambda b, g: (b, g, 0)),
        out_shape=jax.ShapeDtypeStruct((bsz, seq, w), jnp.bfloat16),
        scratch_shapes=[pltpu.VMEM((nblk, tb, w), jnp.bfloat16),
                        pltpu.VMEM((nblk, w, tb), jnp.bfloat16),
                        pltpu.VMEM((SB_Q_GROUP * N_PAIRS, 2 * tb, LANES), jnp.bfloat16),
                        pltpu.VMEM((SB_Q_GROUP, w, tb), jnp.float32),
                        pltpu.VMEM((tb, 2 * tb), jnp.float32),
                        pltpu.VMEM((2, tb, tb), jnp.bfloat16),
                        pltpu.VMEM((SB_Q_GROUP * N_SB_HEADS, SUBLANES, tb), jnp.float32),
                        pltpu.VMEM((tb, tb), jnp.bfloat16)],
        compiler_params=pltpu.CompilerParams(
            dimension_semantics=("arbitrary", "arbitrary"), vmem_limit_bytes=VMEM_LIMIT),
        name="sb_attn",
    )(proj4, proj4, proj4)


def _out_ffn_kernel(x_ref, od_ref, os_ref, mod_ref, wo_ref, gf_ref, w1_ref, w2_ref, gl_ref, o_ref):
    b = pl.program_id(0)
    g_m = mod_ref[2, pl.ds(b, 1), :]
    sh_f = mod_ref[3, pl.ds(b, 1), :]
    sc_f = mod_ref[4, pl.ds(b, 1), :]
    g_f = mod_ref[5, pl.ds(b, 1), :]
    d_ff = w1_ref.shape[1]
    fc = 1024
    tm = x_ref.shape[1]
    halves = [slice(k * (tm // 2), (k + 1) * (tm // 2)) for k in range(2)]
    x1, h, f = [None, None], [None, None], [None, None]
    for k, rows in enumerate(halves):
        mixed = (lax.dot_general(od_ref[0, rows, :], wo_ref[:DIFF_WIDTH, :], _NN, preferred_element_type=jnp.float32)
                 + lax.dot_general(os_ref[0, rows, :], wo_ref[DIFF_WIDTH:, :], _NN,
                                   preferred_element_type=jnp.float32))
        x1[k] = x_ref[0, rows, :] + g_m * mixed
    for k in range(2):
        h[k] = ((x1[k] * _rms_scale(x1[k]) * gf_ref[...]) * (1.0 + sc_f) + sh_f).astype(jnp.bfloat16)
    for c in range(d_ff // fc):
        for k in range(2):
            u = lax.dot_general(h[k], w1_ref[:, c * fc:(c + 1) * fc], _NN, preferred_element_type=jnp.float32)
            r = jnp.square(jnp.maximum(u, 0.0)).astype(jnp.bfloat16)
            part = lax.dot_general(r, w2_ref[c * fc:(c + 1) * fc, :], _NN, preferred_element_type=jnp.float32)
            f[k] = part if f[k] is None else f[k] + part
    for k, rows in enumerate(halves):
        x2 = x1[k] + g_f * f[k]
        o_ref[0, rows, :] = x2 * _rms_scale(x2) * gl_ref[...]


def _out_ffn_call(x, o_diff, o_sb, mod3, wo, gf, w1, w2, gl):
    bsz, seq, d = x.shape
    tm = TOKEN_TILE
    const = dict(pipeline_mode=pl.Buffered(1))
    tok = lambda w: pl.BlockSpec((1, tm, w), lambda b, t: (b, t, 0))
    return pl.pallas_call(
        _out_ffn_kernel,
        grid=(bsz, seq // tm),
        in_specs=[tok(d), tok(DIFF_WIDTH), tok(SB_WIDTH),
                  pl.BlockSpec((N_MOD, bsz, d), lambda b, t: (0, 0, 0)),
                  pl.BlockSpec(wo.shape, lambda b, t: (0, 0), **const),
                  pl.BlockSpec((1, d), lambda b, t: (0, 0)),
                  pl.BlockSpec(w1.shape, lambda b, t: (0, 0), **const),
                  pl.BlockSpec(w2.shape, lambda b, t: (0, 0), **const),
                  pl.BlockSpec((1, d), lambda b, t: (0, 0))],
        out_specs=tok(d),
        out_shape=jax.ShapeDtypeStruct((bsz, seq, d), jnp.float32),
        compiler_params=pltpu.CompilerParams(dimension_semantics=("arbitrary", "arbitrary"),
                                             vmem_limit_bytes=VMEM_LIMIT),
        name="out_ffn",
    )(x, o_diff, o_sb, mod3, wo, gf.reshape(1, d), w1, w2, gl.reshape(1, d))


def _rope_tables(seq_len):
    dim = HEAD_DIM
    inv = (1.0 / (np.float32(ROPE_THETA) ** (np.arange(0, dim, 2, dtype=np.float32) / np.float32(dim)))).astype(np.float32)
    ang = np.arange(seq_len, dtype=np.float32)[:, None] * inv[None, :]
    ang = np.concatenate([ang, ang], axis=-1)
    cos, sin = np.cos(ang).astype(np.float32), np.sin(ang).astype(np.float32)
    sign = np.where(np.arange(dim) < dim // 2, -1.0, 1.0).astype(np.float32)
    reps = LANES // dim
    return jnp.asarray(np.tile(cos, (1, reps))), jnp.asarray(np.tile(sin * sign[None, :], (1, reps)))


def kernel(x, c, ada_w, ada_b, mix_norm_g, w_in, lambda_q1, lambda_k1, lambda_q2, lambda_k2,
           diff_subln_g, w_out, ffn_norm_g, w_ff1, w_ff2, final_norm_g):
    bsz, seq, d = x.shape
    depth = ada_w.shape[0]
    assert depth == 1, "the fused final norm assumes a single layer"
    assert seq % TOKEN_TILE == 0 and seq % ATT_BLOCK == 0
    cos128, sin128 = _rope_tables(seq)
    layer = 0
    lambda_init = 0.8 - 0.6 * math.exp(-0.3 * layer)
    mod3 = _adaln_call(c, ada_w[layer], ada_b[layer])
    proj = _inproj_call(x, mod3, mix_norm_g[layer], w_in[layer], cos128, sin128)
    proj4 = proj.reshape(bsz, seq // ATT_BLOCK, ATT_BLOCK, proj.shape[-1])
    o_diff, wo_bf, w1_bf, w2_bf = _diff_attn_call(
        proj4, lambda_q1[layer], lambda_k1[layer], lambda_q2[layer], lambda_k2[layer], diff_subln_g[layer],
        lambda_init, w_out[layer], w_ff1[layer], w_ff2[layer])
    o_sb = _sb_attn_call(proj4)
    return _out_ffn_call(x, o_diff, o_sb, mod3, wo_bf, ffn_norm_g[layer], w1_bf, w2_bf, final_norm_g)
```

```python
import functools
import math

import jax
import jax.numpy as jnp
import numpy as np
from jax import lax
from jax.experimental import pallas as pl
from jax.experimental.pallas import tpu as pltpu

HEAD_DIM = 64
N_DIFF_HEADS = 4
N_SB_HEADS = 8
DIFF_WIDTH = N_DIFF_HEADS * 2 * HEAD_DIM
SB_WIDTH = N_SB_HEADS * HEAD_DIM
CHUNK = 64
ROPE_THETA = 10000.0
NORM_EPS = 1e-6
N_MOD = 6

LANES = 128
SUBLANES = 8
ATT_BLOCK = 256
SEG_LEN = ATT_BLOCK // SUBLANES
TOKEN_TILE = 512
N_PAIRS = N_SB_HEADS // 2
Q_GROUP = 2
ONES_ROWS = 16
LOG2E = 1.4426950408889634
VMEM_LIMIT = 56 * 1024 * 1024

_NT = (((1,), (1,)), ((), ()))
_NN = (((1,), (0,)), ((), ()))


def _rms_scale(xf):
    return lax.rsqrt(jnp.mean(xf * xf, axis=-1, keepdims=True) + NORM_EPS)


def _adaln_kernel(c_ref, w_ref, b_ref, o_ref):
    c = c_ref[...]
    ca = c / (1.0 + jnp.exp(-c))
    o_ref[0] = lax.dot_general(ca.astype(jnp.bfloat16), w_ref[...].astype(jnp.bfloat16), _NN,
                               preferred_element_type=jnp.float32) + b_ref[...]


def _adaln_call(c, w, b):
    bsz, d = c.shape
    n = w.shape[1]
    tn = d
    return pl.pallas_call(
        _adaln_kernel,
        grid=(n // tn,),
        in_specs=[pl.BlockSpec((bsz, d), lambda j: (0, 0)),
                  pl.BlockSpec((d, tn), lambda j: (0, j)),
                  pl.BlockSpec((1, tn), lambda j: (0, j))],
        out_specs=pl.BlockSpec((1, bsz, tn), lambda j: (j, 0, 0)),
        out_shape=jax.ShapeDtypeStruct((n // tn, bsz, tn), jnp.float32),
        compiler_params=pltpu.CompilerParams(dimension_semantics=("arbitrary",),
                                             vmem_limit_bytes=VMEM_LIMIT),
        name="adaln",
    )(c, w, b.reshape(1, n))


def _inproj_kernel(x_ref, mod_ref, g_ref, w_ref, cos_ref, sin_ref, o_ref, wb_scr):
    b = pl.program_id(0)
    gw = DIFF_WIDTH

    @pl.when((b == 0) & (pl.program_id(1) == 0))
    def _():
        for n in range(w_ref.shape[1] // gw):
            wb_scr[:, n * gw:(n + 1) * gw] = w_ref[:, n * gw:(n + 1) * gw].astype(wb_scr.dtype)

    xf = x_ref[0]
    sh = mod_ref[0, pl.ds(b, 1), :]
    sc = mod_ref[1, pl.ds(b, 1), :]
    h = (xf * _rms_scale(xf) * g_ref[...]) * (1.0 + sc) + sh
    hb = h.astype(jnp.bfloat16)
    cos = cos_ref[...]
    sin = sin_ref[...]
    lane = lax.broadcasted_iota(jnp.int32, cos.shape, 1)
    first_half = (lane % HEAD_DIM) < (HEAD_DIM // 2)
    scale = HEAD_DIM ** -0.5
    for n in range(6):
        acc = lax.dot_general(hb, wb_scr[:, n * gw:(n + 1) * gw], _NN,
                              preferred_element_type=jnp.float32)
        if n < 2:
            for j in range(gw // LANES):
                xb = acc[:, j * LANES:(j + 1) * LANES]
                rot = jnp.where(first_half, pltpu.roll(xb, LANES - HEAD_DIM // 2, 1),
                                pltpu.roll(xb, HEAD_DIM // 2, 1))
                yb = xb * cos + rot * sin
                if n == 0:
                    yb = yb * (scale * LOG2E)
                o_ref[0, :, n * gw + j * LANES:n * gw + (j + 1) * LANES] = yb.astype(o_ref.dtype)
        else:
            if n == 3:
                acc = acc * (0.5 * scale)
            o_ref[0, :, n * gw:(n + 1) * gw] = acc.astype(o_ref.dtype)


def _inproj_call(x, mod3, g, w, cos128, sin128):
    bsz, seq, d = x.shape
    n = w.shape[1]
    tm = TOKEN_TILE
    const = dict(pipeline_mode=pl.Buffered(1))
    return pl.pallas_call(
        _inproj_kernel,
        grid=(bsz, seq // tm),
        in_specs=[pl.BlockSpec((1, tm, d), lambda b, t: (b, t, 0)),
                  pl.BlockSpec((N_MOD, bsz, d), lambda b, t: (0, 0, 0)),
                  pl.BlockSpec((1, d), lambda b, t: (0, 0)),
                  pl.BlockSpec((d, n), lambda b, t: (0, 0), **const),
                  pl.BlockSpec((tm, LANES), lambda b, t: (t, 0)),
                  pl.BlockSpec((tm, LANES), lambda b, t: (t, 0))],
        out_specs=pl.BlockSpec((1, tm, n), lambda b, t: (b, t, 0)),
        out_shape=jax.ShapeDtypeStruct((bsz, seq, n), jnp.bfloat16),
        scratch_shapes=[pltpu.VMEM((d, n), jnp.bfloat16)],
        compiler_params=pltpu.CompilerParams(dimension_semantics=("arbitrary", "arbitrary"),
                                             vmem_limit_bytes=VMEM_LIMIT),
        name="inproj",
    )(x, mod3, g.reshape(1, d), w, cos128, sin128)


def _attn_kernel(lq1_ref, lk1_ref, lq2_ref, lk2_ref, g_ref, dq_ref, dk_ref, dv_ref, sq_ref, sk_ref, sv_ref,
                 wo_ref, w1_ref, w2_ref, od_ref, os_ref, wo_bf_ref, w1_bf_ref, w2_bf_ref,
                 dvt_scr, dqm_scr, dacc_scr, m_scr, s_scr, p_scr, alpha_scr,
                 kp_scr, svt_scr, sqm_scr, sacc_scr, pre_scr, ap_scr, carry_scr, *, lambda_init):
    g = pl.program_id(1)
    nblk = dk_ref.shape[1]
    tb = ATT_BLOCK
    tw = 2 * tb
    dv = 2 * HEAD_DIM
    vrows = dv + ONES_ROWS
    nd = N_DIFF_HEADS
    ns = N_SB_HEADS
    d_last = Q_GROUP * nd - 1

    @pl.when(g == 0)
    def _():
        for blk in range(nblk):
            vt = dv_ref[0, blk].astype(jnp.float32).T.astype(jnp.bfloat16)
            for hd in range(nd):
                dvt_scr[blk, hd * vrows:hd * vrows + dv, :] = vt[hd * dv:(hd + 1) * dv, :]
                dvt_scr[blk, hd * vrows + dv:(hd + 1) * vrows, :] = jnp.ones((ONES_ROWS, tb), jnp.bfloat16)
        row = lax.broadcasted_iota(jnp.int32, (tb, tb), 0)
        col = lax.broadcasted_iota(jnp.int32, (tb, tb), 1)
        pos = SEG_LEN * (row % SUBLANES) + (SEG_LEN - 1 - row // SUBLANES)
        perm = jnp.where(col == pos, 1.0, 0.0).astype(jnp.bfloat16)
        for blk in range(nblk):
            kp = lax.dot_general(perm, sk_ref[0, blk], _NN, preferred_element_type=jnp.float32)
            kp_scr[blk] = kp.astype(jnp.bfloat16)
            vp = lax.dot_general(perm, sv_ref[0, blk], _NN, preferred_element_type=jnp.float32)
            svt_scr[blk] = vp.T.astype(jnp.bfloat16)

    wo_bf_ref[...] = wo_ref[...].astype(wo_bf_ref.dtype)
    w1_bf_ref[...] = w1_ref[...].astype(w1_bf_ref.dtype)
    w2_bf_ref[...] = w2_ref[...].astype(w2_bf_ref.dtype)

    lane = lax.broadcasted_iota(jnp.int32, (tb, LANES), 1)
    for qb in range(Q_GROUP):
        for c in range(nd):
            for ref, scr in ((dq_ref, dqm_scr), (sq_ref, sqm_scr)):
                qp = ref[0, qb, :, c * LANES:(c + 1) * LANES]
                zero = jnp.zeros_like(qp)
                scr[qb * nd + c, :tb, :] = jnp.where(lane < HEAD_DIM, qp, zero)
                scr[qb * nd + c, tb:, :] = jnp.where(lane >= HEAD_DIM, qp, zero)
        for h in range(ns):
            carry_scr[qb * ns + h] = jnp.ones((SUBLANES, tb), jnp.float32)
    dacc_scr[d_last] = jnp.zeros((vrows, tw), jnp.float32)
    alpha_scr[...] = jnp.zeros((1, tw), jnp.float32)

    sub = lax.broadcasted_iota(jnp.int32, (SUBLANES, tb), 0)
    qcol = lax.broadcasted_iota(jnp.int32, (SUBLANES, tb), 1)
    qcol2 = lax.broadcasted_iota(jnp.int32, (SUBLANES, tw), 1) % tb

    def d_scores(j, qb, hd):
        return lax.dot_general(dk_ref[0, j, :, hd * LANES:(hd + 1) * LANES], dqm_scr[qb * nd + hd], _NT,
                               preferred_element_type=jnp.float32)

    def d_elementwise(s_t, qb, hd, first):
        buf = hd % 2
        it = qb * nd + hd
        mx = None
        for v in range(tb // SUBLANES):
            sv = s_t[v * SUBLANES:(v + 1) * SUBLANES, :]
            if first:
                sv = jnp.where(qcol2 >= (v * SUBLANES // CHUNK) * CHUNK, sv, -jnp.inf)
            mx = sv if mx is None else jnp.maximum(mx, sv)
            s_scr[buf, v * SUBLANES:(v + 1) * SUBLANES, :] = sv
        tmax = jnp.max(mx, axis=0, keepdims=True)
        if first:
            m_new = tmax
            alpha = None
        else:
            m_old = m_scr[it]
            m_new = jnp.maximum(m_old, tmax)
            alpha = jnp.exp2(m_old - m_new)
        m_scr[it] = m_new
        return jnp.exp2(s_scr[buf] - m_new).astype(jnp.bfloat16), alpha

    def d_pv(j, hd, p):
        return lax.dot_general(dvt_scr[j, hd * vrows:(hd + 1) * vrows, :], p, _NN,
                               preferred_element_type=jnp.float32)

    def finish_deferred(j_prev):
        dacc_scr[d_last] = alpha_scr[...] * dacc_scr[d_last] + d_pv(j_prev, nd - 1, p_scr[...])

    def s_scores(j, qb, pair):
        return lax.dot_general(kp_scr[j, :, pair * LANES:(pair + 1) * LANES], sqm_scr[qb * N_PAIRS + pair], _NT,
                               preferred_element_type=jnp.float32)

    def s_elementwise(zh, qb, h, diag):
        buf = h % 2
        rem = jnp.ones((SUBLANES, tb), jnp.float32)
        prev = None
        for v in range(SEG_LEN):
            beta = 0.5 * jnp.tanh(zh[v * SUBLANES:(v + 1) * SUBLANES, :]) + 0.5
            if diag:
                beta = jnp.where(SEG_LEN * sub + (SEG_LEN - 1 - v) < qcol, beta, 0.0)
            a_v = beta * rem
            rem = rem - a_v
            if v % 2 == 0:
                prev = a_v
            else:
                ap_scr[buf, (v - 1) * SUBLANES:(v + 1) * SUBLANES, :] = jnp.concatenate(
                    [prev, a_v], axis=0).astype(jnp.bfloat16)
        incl = rem
        for k in (1, 2, 4):
            shifted = pltpu.roll(incl, SUBLANES - k, 0)
            incl = incl * jnp.where(sub + k < SUBLANES, shifted, 1.0)
        excl = jnp.where(sub + 1 < SUBLANES, pltpu.roll(incl, SUBLANES - 1, 0), 1.0)
        carry = carry_scr[qb * ns + h]
        start = carry * excl
        carry_scr[qb * ns + h] = carry * jnp.broadcast_to(incl[0:1, :], carry.shape)
        start2 = jnp.concatenate([start, start], axis=0).astype(jnp.bfloat16)
        return jnp.concatenate(
            [ap_scr[buf, u * 2 * SUBLANES:(u + 1) * 2 * SUBLANES, :] * start2 for u in range(SEG_LEN // 2)],
            axis=0)

    def s_av(j, h, a):
        return lax.dot_general(svt_scr[j, h * HEAD_DIM:(h + 1) * HEAD_DIM, :], a, _NN,
                               preferred_element_type=jnp.float32)

    def tile_body(j, blocks, nxt, j_prev):
        if j_prev is not None:
            finish_deferred(j_prev)
        entries = []
        for qb, masked in blocks:
            for c in range(nd):
                entries.append(("s", qb, c, masked))
                entries.append(("d", qb, c, masked))
        sc = [None] * len(entries)
        sc[0] = pre_scr[...]
        issued = 1

        def issue(upto):
            nonlocal issued
            while issued <= upto:
                if issued < len(entries):
                    kind, qb, c, _ = entries[issued]
                    sc[issued] = s_scores(j, qb, c) if kind == "s" else d_scores(j, qb, c)
                elif issued == len(entries):
                    pre_scr[...] = s_scores(nxt[0], nxt[1], 0)
                issued += 1

        for k, (kind, qb, c, masked) in enumerate(entries):
            issue(k + 2)
            if kind == "s":
                for hh in range(2):
                    h = 2 * c + hh
                    a = s_elementwise(sc[k][:, hh * tb:(hh + 1) * tb], qb, h, masked)
                    rows = slice(h * HEAD_DIM, (h + 1) * HEAD_DIM)
                    if masked:
                        sacc_scr[qb, rows, :] = s_av(j, h, a)
                    else:
                        sacc_scr[qb, rows, :] += s_av(j, h, a)
            else:
                p, alpha = d_elementwise(sc[k], qb, c, masked)
                it = qb * nd + c
                if it == d_last:
                    p_scr[...] = p
                    if not masked:
                        alpha_scr[...] = alpha
                elif masked:
                    dacc_scr[it] = d_pv(j, c, p)
                else:
                    dacc_scr[it] = alpha * dacc_scr[it] + d_pv(j, c, p)
            sc[k] = None

    base = g * Q_GROUP
    pre_scr[...] = s_scores(base + Q_GROUP - 1, Q_GROUP - 1, 0)
    j_prev = None
    for d in range(Q_GROUP - 1, -1, -1):
        blocks = [(d, True)] + [(qb, False) for qb in range(d + 1, Q_GROUP)]
        nxt = (base + d - 1, d - 1) if d > 0 else (jnp.maximum(base - 1, 0), 0)
        tile_body(base + d, blocks, nxt, j_prev)
        j_prev = base + d

    def body(t, c):
        j = base - 1 - t
        tile_body(j, [(qb, False) for qb in range(Q_GROUP)], (jnp.maximum(j - 1, 0), 0), j + 1)
        return c

    if nblk > Q_GROUP:
        lax.fori_loop(0, base, body, 0)
    finish_deferred(0)

    g_col = jnp.broadcast_to(g_ref[...], (dv, dv)).T
    g_col = jnp.concatenate([g_col] * (tb // dv), axis=1)
    lam = (jnp.exp(jnp.sum(lq1_ref[...] * lk1_ref[...], axis=-1, keepdims=True))
           - jnp.exp(jnp.sum(lq2_ref[...] * lk2_ref[...], axis=-1, keepdims=True)) + lambda_init)
    for qb in range(Q_GROUP):
        for hd in range(nd):
            a1 = dacc_scr[qb * nd + hd, :, :tb]
            a2 = dacc_scr[qb * nd + hd, :, tb:]
            o_t = a1[:dv] / a1[dv:dv + 1] - lam * (a2[:dv] / a2[dv:dv + 1])
            ms = jnp.mean(o_t * o_t, axis=0, keepdims=True)
            o_t = o_t * lax.rsqrt(ms + NORM_EPS) * g_col * (1.0 - lambda_init)
            od_ref[0, qb * tb:(qb + 1) * tb, hd * LANES:(hd + 1) * LANES] = o_t.T.astype(od_ref.dtype)
        os_ref[0, qb * tb:(qb + 1) * tb, :] = sacc_scr[qb].T.astype(os_ref.dtype)


def _attn_call(proj4, lq1, lk1, lq2, lk2, subln_g, lambda_init, w_out, w_ff1, w_ff2):
    bsz, nblk, tb, _ = proj4.shape
    seq = nblk * tb
    w = DIFF_WIDTH
    assert SB_WIDTH == w and nblk % Q_GROUP == 0
    gsteps = nblk // Q_GROUP
    nsteps = bsz * gsteps

    def slab(arr):
        rows = arr.shape[0] // nsteps
        assert rows * nsteps == arr.shape[0] and rows % (2 * SUBLANES) == 0
        return pl.BlockSpec((rows, arr.shape[1]), lambda b, g: (b * gsteps + g, 0))

    wspecs = [slab(w_out), slab(w_ff1), slab(w_ff2)]
    wshapes = [jax.ShapeDtypeStruct(a.shape, jnp.bfloat16) for a in (w_out, w_ff1, w_ff2)]
    vrows = 2 * HEAD_DIM + ONES_ROWS
    nit = Q_GROUP * N_DIFF_HEADS
    vec = lambda a: a.reshape(1, HEAD_DIM)
    small = pl.BlockSpec((1, HEAD_DIM), lambda b, g: (0, 0))
    qspec = lambda col: pl.BlockSpec((1, Q_GROUP, tb, w), lambda b, g: (b, g, 0, col))
    kvspec = lambda col: pl.BlockSpec((1, nblk, tb, w), lambda b, g: (b, 0, 0, col))
    ospec = pl.BlockSpec((1, Q_GROUP * tb, w), lambda b, g: (b, g, 0))
    oshape = jax.ShapeDtypeStruct((bsz, seq, w), jnp.bfloat16)
    return pl.pallas_call(
        functools.partial(_attn_kernel, lambda_init=lambda_init),
        grid=(bsz, nblk // Q_GROUP),
        in_specs=[small, small, small, small,
                  pl.BlockSpec((1, 2 * HEAD_DIM), lambda b, g: (0, 0)),
                  qspec(0), kvspec(1), kvspec(2), qspec(3), kvspec(4), kvspec(5)] + wspecs,
        out_specs=[ospec, ospec] + wspecs,
        out_shape=[oshape, oshape] + wshapes,
        scratch_shapes=[pltpu.VMEM((nblk, N_DIFF_HEADS * vrows, tb), jnp.bfloat16),
                        pltpu.VMEM((nit, 2 * tb, LANES), jnp.bfloat16),
                        pltpu.VMEM((nit, vrows, 2 * tb), jnp.float32),
                        pltpu.VMEM((nit, 1, 2 * tb), jnp.float32),
                        pltpu.VMEM((2, tb, 2 * tb), jnp.float32),
                        pltpu.VMEM((tb, 2 * tb), jnp.bfloat16),
                        pltpu.VMEM((1, 2 * tb), jnp.float32),
                        pltpu.VMEM((nblk, tb, w), jnp.bfloat16),
                        pltpu.VMEM((nblk, w, tb), jnp.bfloat16),
                        pltpu.VMEM((Q_GROUP * N_PAIRS, 2 * tb, LANES), jnp.bfloat16),
                        pltpu.VMEM((Q_GROUP, w, tb), jnp.float32),
                        pltpu.VMEM((tb, 2 * tb), jnp.float32),
                        pltpu.VMEM((2, tb, tb), jnp.bfloat16),
                        pltpu.VMEM((Q_GROUP * N_SB_HEADS, SUBLANES, tb), jnp.float32)],
        compiler_params=pltpu.CompilerParams(
            dimension_semantics=("arbitrary", "arbitrary"), vmem_limit_bytes=VMEM_LIMIT),
        name="attn",
    )(vec(lq1), vec(lk1), vec(lq2), vec(lk2), subln_g.reshape(1, 2 * HEAD_DIM),
      proj4, proj4, proj4, proj4, proj4, proj4, w_out, w_ff1, w_ff2)


def _out_ffn_kernel(x_ref, od_ref, os_ref, mod_ref, wo_ref, gf_ref, w1_ref, w2_ref, gl_ref, o_ref):
    b = pl.program_id(0)
    g_m = mod_ref[2, pl.ds(b, 1), :]
    sh_f = mod_ref[3, pl.ds(b, 1), :]
    sc_f = mod_ref[4, pl.ds(b, 1), :]
    g_f = mod_ref[5, pl.ds(b, 1), :]
    d_ff = w1_ref.shape[1]
    fc = 1024
    tm = x_ref.shape[1]
    halves = [slice(k * (tm // 2), (k + 1) * (tm // 2)) for k in range(2)]
    x1, h, f = [None, None], [None, None], [None, None]
    for k, rows in enumerate(halves):
        mixed = (lax.dot_general(od_ref[0, rows, :], wo_ref[:DIFF_WIDTH, :], _NN, preferred_element_type=jnp.float32)
                 + lax.dot_general(os_ref[0, rows, :], wo_ref[DIFF_WIDTH:, :], _NN,
                                   preferred_element_type=jnp.float32))
        x1[k] = x_ref[0, rows, :] + g_m * mixed
    for k in range(2):
        h[k] = ((x1[k] * _rms_scale(x1[k]) * gf_ref[...]) * (1.0 + sc_f) + sh_f).astype(jnp.bfloat16)
    for c in range(d_ff // fc):
        for k in range(2):
            u = lax.dot_general(h[k], w1_ref[:, c * fc:(c + 1) * fc], _NN, preferred_element_type=jnp.float32)
            r = jnp.square(jnp.maximum(u, 0.0)).astype(jnp.bfloat16)
            part = lax.dot_general(r, w2_ref[c * fc:(c + 1) * fc, :], _NN, preferred_element_type=jnp.float32)
            f[k] = part if f[k] is None else f[k] + part
    for k, rows in enumerate(halves):
        x2 = x1[k] + g_f * f[k]
        o_ref[0, rows, :] = x2 * _rms_scale(x2) * gl_ref[...]


def _out_ffn_call(x, o_diff, o_sb, mod3, wo, gf, w1, w2, gl):
    bsz, seq, d = x.shape
    tm = TOKEN_TILE
    const = dict(pipeline_mode=pl.Buffered(1))
    tok = lambda w: pl.BlockSpec((1, tm, w), lambda b, t: (b, t, 0))
    return pl.pallas_call(
        _out_ffn_kernel,
        grid=(bsz, seq // tm),
        in_specs=[tok(d), tok(DIFF_WIDTH), tok(SB_WIDTH),
                  pl.BlockSpec((N_MOD, bsz, d), lambda b, t: (0, 0, 0)),
                  pl.BlockSpec(wo.shape, lambda b, t: (0, 0), **const),
                  pl.BlockSpec((1, d), lambda b, t: (0, 0)),
                  pl.BlockSpec(w1.shape, lambda b, t: (0, 0), **const),
                  pl.BlockSpec(w2.shape, lambda b, t: (0, 0), **const),
                  pl.BlockSpec((1, d), lambda b, t: (0, 0))],
        out_specs=tok(d),
        out_shape=jax.ShapeDtypeStruct((bsz, seq, d), jnp.float32),
        compiler_params=pltpu.CompilerParams(dimension_semantics=("arbitrary", "arbitrary"),
                                             vmem_limit_bytes=VMEM_LIMIT),
        name="out_ffn",
    )(x, o_diff, o_sb, mod3, wo, gf.reshape(1, d), w1, w2, gl.reshape(1, d))


def _rope_tables(seq_len):
    dim = HEAD_DIM
    inv = (1.0 / (np.float32(ROPE_THETA) ** (np.arange(0, dim, 2, dtype=np.float32) / np.float32(dim)))).astype(np.float32)
    ang = np.arange(seq_len, dtype=np.float32)[:, None] * inv[None, :]
    ang = np.concatenate([ang, ang], axis=-1)
    cos, sin = np.cos(ang).astype(np.float32), np.sin(ang).astype(np.float32)
    sign = np.where(np.arange(dim) < dim // 2, -1.0, 1.0).astype(np.float32)
    reps = LANES // dim
    return jnp.asarray(np.tile(cos, (1, reps))), jnp.asarray(np.tile(sin * sign[None, :], (1, reps)))


def kernel(x, c, ada_w, ada_b, mix_norm_g, w_in, lambda_q1, lambda_k1, lambda_q2, lambda_k2,
           diff_subln_g, w_out, ffn_norm_g, w_ff1, w_ff2, final_norm_g):
    bsz, seq, d = x.shape
    depth = ada_w.shape[0]
    assert depth == 1, "the fused final norm assumes a single layer"
    assert seq % TOKEN_TILE == 0 and seq % ATT_BLOCK == 0
    cos128, sin128 = _rope_tables(seq)
    layer = 0
    lambda_init = 0.8 - 0.6 * math.exp(-0.3 * layer)
    mod3 = _adaln_call(c, ada_w[layer], ada_b[layer])
    proj = _inproj_call(x, mod3, mix_norm_g[layer], w_in[layer], cos128, sin128)
    proj4 = proj.reshape(bsz, seq // ATT_BLOCK, ATT_BLOCK, proj.shape[-1])
    o_diff, o_sb, wo_bf, w1_bf, w2_bf = _attn_call(
        proj4, lambda_q1[layer], lambda_k1[layer], lambda_q2[layer], lambda_k2[layer], diff_subln_g[layer],
        lambda_init, w_out[layer], w_ff1[layer], w_ff2[layer])
    return _out_ffn_call(x, o_diff, o_sb, mod3, wo_bf, ffn_norm_g[layer], w1_bf, w2_bf, final_norm_g)
```

```python
import functools
import math

import jax
import jax.numpy as jnp
import numpy as np
from jax import lax
from jax.experimental import pallas as pl
from jax.experimental.pallas import tpu as pltpu

HEAD_DIM = 64
N_DIFF_HEADS = 4
N_SB_HEADS = 8
DIFF_WIDTH = N_DIFF_HEADS * 2 * HEAD_DIM
SB_WIDTH = N_SB_HEADS * HEAD_DIM
CHUNK = 64
ROPE_THETA = 10000.0
NORM_EPS = 1e-6
N_MOD = 6

LANES = 128
SUBLANES = 8
ATT_BLOCK = 256
SEG_LEN = ATT_BLOCK // SUBLANES
TOKEN_TILE = 512
N_PAIRS = N_SB_HEADS // 2
Q_GROUP = 4
ONES_ROWS = 16
LOG2E = 1.4426950408889634
VMEM_LIMIT = 56 * 1024 * 1024

_NT = (((1,), (1,)), ((), ()))
_NN = (((1,), (0,)), ((), ()))


def _rms_scale(xf):
    return lax.rsqrt(jnp.mean(xf * xf, axis=-1, keepdims=True) + NORM_EPS)


def _adaln_kernel(c_ref, w_ref, b_ref, o_ref):
    c = c_ref[...]
    ca = c / (1.0 + jnp.exp(-c))
    o_ref[0] = lax.dot_general(ca.astype(jnp.bfloat16), w_ref[...].astype(jnp.bfloat16), _NN,
                               preferred_element_type=jnp.float32) + b_ref[...]


def _adaln_call(c, w, b):
    bsz, d = c.shape
    n = w.shape[1]
    tn = d
    return pl.pallas_call(
        _adaln_kernel,
        grid=(n // tn,),
        in_specs=[pl.BlockSpec((bsz, d), lambda j: (0, 0)),
                  pl.BlockSpec((d, tn), lambda j: (0, j)),
                  pl.BlockSpec((1, tn), lambda j: (0, j))],
        out_specs=pl.BlockSpec((1, bsz, tn), lambda j: (j, 0, 0)),
        out_shape=jax.ShapeDtypeStruct((n // tn, bsz, tn), jnp.float32),
        compiler_params=pltpu.CompilerParams(dimension_semantics=("arbitrary",),
                                             vmem_limit_bytes=VMEM_LIMIT),
        name="adaln",
    )(c, w, b.reshape(1, n))


def _inproj_kernel(x_ref, mod_ref, g_ref, w_ref, cos_ref, sin_ref, o_ref, wb_scr):
    b = pl.program_id(0)
    gw = DIFF_WIDTH

    @pl.when((b == 0) & (pl.program_id(1) == 0))
    def _():
        for n in range(w_ref.shape[1] // gw):
            wb_scr[:, n * gw:(n + 1) * gw] = w_ref[:, n * gw:(n + 1) * gw].astype(wb_scr.dtype)

    xf = x_ref[0]
    sh = mod_ref[0, pl.ds(b, 1), :]
    sc = mod_ref[1, pl.ds(b, 1), :]
    h = (xf * _rms_scale(xf) * g_ref[...]) * (1.0 + sc) + sh
    hb = h.astype(jnp.bfloat16)
    cos = cos_ref[...]
    sin = sin_ref[...]
    lane = lax.broadcasted_iota(jnp.int32, cos.shape, 1)
    first_half = (lane % HEAD_DIM) < (HEAD_DIM // 2)
    scale = HEAD_DIM ** -0.5
    for n in range(6):
        acc = lax.dot_general(hb, wb_scr[:, n * gw:(n + 1) * gw], _NN,
                              preferred_element_type=jnp.float32)
        if n < 2:
            for j in range(gw // LANES):
                xb = acc[:, j * LANES:(j + 1) * LANES]
                rot = jnp.where(first_half, pltpu.roll(xb, LANES - HEAD_DIM // 2, 1),
                                pltpu.roll(xb, HEAD_DIM // 2, 1))
                yb = xb * cos + rot * sin
                if n == 0:
                    yb = yb * (scale * LOG2E)
                o_ref[0, :, n * gw + j * LANES:n * gw + (j + 1) * LANES] = yb.astype(o_ref.dtype)
        else:
            if n == 3:
                acc = acc * (0.5 * scale)
            o_ref[0, :, n * gw:(n + 1) * gw] = acc.astype(o_ref.dtype)


def _inproj_call(x, mod3, g, w, cos128, sin128):
    bsz, seq, d = x.shape
    n = w.shape[1]
    tm = TOKEN_TILE
    const = dict(pipeline_mode=pl.Buffered(1))
    return pl.pallas_call(
        _inproj_kernel,
        grid=(bsz, seq // tm),
        in_specs=[pl.BlockSpec((1, tm, d), lambda b, t: (b, t, 0)),
                  pl.BlockSpec((N_MOD, bsz, d), lambda b, t: (0, 0, 0)),
                  pl.BlockSpec((1, d), lambda b, t: (0, 0)),
                  pl.BlockSpec((d, n), lambda b, t: (0, 0), **const),
                  pl.BlockSpec((tm, LANES), lambda b, t: (t, 0)),
                  pl.BlockSpec((tm, LANES), lambda b, t: (t, 0))],
        out_specs=pl.BlockSpec((1, tm, n), lambda b, t: (b, t, 0)),
        out_shape=jax.ShapeDtypeStruct((bsz, seq, n), jnp.bfloat16),
        scratch_shapes=[pltpu.VMEM((d, n), jnp.bfloat16)],
        compiler_params=pltpu.CompilerParams(dimension_semantics=("arbitrary", "arbitrary"),
                                             vmem_limit_bytes=VMEM_LIMIT),
        name="inproj",
    )(x, mod3, g.reshape(1, d), w, cos128, sin128)


def _attn_kernel(lq1_ref, lk1_ref, lq2_ref, lk2_ref, g_ref, dq_ref, dk_ref, dv_ref, sq_ref, sk_ref, sv_ref,
                 wo_ref, w1_ref, w2_ref, od_ref, os_ref, wo_bf_ref, w1_bf_ref, w2_bf_ref,
                 dvt_scr, dqm_scr, dacc_scr, m_scr, s_scr, p_scr, alpha_scr,
                 kp_scr, svt_scr, sqm_scr, sacc_scr, pre_scr, ap_scr, carry_scr, *, lambda_init):
    g = pl.program_id(1)
    nblk = dk_ref.shape[1]
    tb = ATT_BLOCK
    tw = 2 * tb
    dv = 2 * HEAD_DIM
    vrows = dv + ONES_ROWS
    nd = N_DIFF_HEADS
    ns = N_SB_HEADS
    d_last = Q_GROUP * nd - 1

    @pl.when(g == 0)
    def _():
        for blk in range(nblk):
            vt = dv_ref[0, blk].astype(jnp.float32).T.astype(jnp.bfloat16)
            for hd in range(nd):
                dvt_scr[blk, hd * vrows:hd * vrows + dv, :] = vt[hd * dv:(hd + 1) * dv, :]
                dvt_scr[blk, hd * vrows + dv:(hd + 1) * vrows, :] = jnp.ones((ONES_ROWS, tb), jnp.bfloat16)
        row = lax.broadcasted_iota(jnp.int32, (tb, tb), 0)
        col = lax.broadcasted_iota(jnp.int32, (tb, tb), 1)
        pos = SEG_LEN * (row % SUBLANES) + (SEG_LEN - 1 - row // SUBLANES)
        perm = jnp.where(col == pos, 1.0, 0.0).astype(jnp.bfloat16)
        for blk in range(nblk):
            kp = lax.dot_general(perm, sk_ref[0, blk], _NN, preferred_element_type=jnp.float32)
            kp_scr[blk] = kp.astype(jnp.bfloat16)
            vp = lax.dot_general(perm, sv_ref[0, blk], _NN, preferred_element_type=jnp.float32)
            svt_scr[blk] = vp.T.astype(jnp.bfloat16)

    wo_bf_ref[...] = wo_ref[...].astype(wo_bf_ref.dtype)
    w1_bf_ref[...] = w1_ref[...].astype(w1_bf_ref.dtype)
    w2_bf_ref[...] = w2_ref[...].astype(w2_bf_ref.dtype)

    lane = lax.broadcasted_iota(jnp.int32, (tb, LANES), 1)
    for qb in range(Q_GROUP):
        for c in range(nd):
            for ref, scr in ((dq_ref, dqm_scr), (sq_ref, sqm_scr)):
                qp = ref[0, qb, :, c * LANES:(c + 1) * LANES]
                zero = jnp.zeros_like(qp)
                scr[qb * nd + c, :tb, :] = jnp.where(lane < HEAD_DIM, qp, zero)
                scr[qb * nd + c, tb:, :] = jnp.where(lane >= HEAD_DIM, qp, zero)
        for h in range(ns):
            carry_scr[qb * ns + h] = jnp.ones((SUBLANES, tb), jnp.float32)
    dacc_scr[d_last] = jnp.zeros((vrows, tw), jnp.float32)
    alpha_scr[...] = jnp.zeros((1, tw), jnp.float32)

    sub = lax.broadcasted_iota(jnp.int32, (SUBLANES, tb), 0)
    qcol = lax.broadcasted_iota(jnp.int32, (SUBLANES, tb), 1)
    qcol2 = lax.broadcasted_iota(jnp.int32, (SUBLANES, tw), 1) % tb

    def d_scores(j, qb, hd):
        return lax.dot_general(dk_ref[0, j, :, hd * LANES:(hd + 1) * LANES], dqm_scr[qb * nd + hd], _NT,
                               preferred_element_type=jnp.float32)

    def d_elementwise(s_t, qb, hd, first):
        buf = hd % 2
        it = qb * nd + hd
        mx = None
        for v in range(tb // SUBLANES):
            sv = s_t[v * SUBLANES:(v + 1) * SUBLANES, :]
            if first:
                sv = jnp.where(qcol2 >= (v * SUBLANES // CHUNK) * CHUNK, sv, -jnp.inf)
            mx = sv if mx is None else jnp.maximum(mx, sv)
            s_scr[buf, v * SUBLANES:(v + 1) * SUBLANES, :] = sv
        tmax = jnp.max(mx, axis=0, keepdims=True)
        if first:
            m_new = tmax
            alpha = None
        else:
            m_old = m_scr[it]
            m_new = jnp.maximum(m_old, tmax)
            alpha = jnp.exp2(m_old - m_new)
        m_scr[it] = m_new
        return jnp.exp2(s_scr[buf] - m_new).astype(jnp.bfloat16), alpha

    def d_pv(j, hd, p):
        return lax.dot_general(dvt_scr[j, hd * vrows:(hd + 1) * vrows, :], p, _NN,
                               preferred_element_type=jnp.float32)

    def finish_deferred(j_prev):
        dacc_scr[d_last] = alpha_scr[...] * dacc_scr[d_last] + d_pv(j_prev, nd - 1, p_scr[...])

    def s_scores(j, qb, pair):
        return lax.dot_general(kp_scr[j, :, pair * LANES:(pair + 1) * LANES], sqm_scr[qb * N_PAIRS + pair], _NT,
                               preferred_element_type=jnp.float32)

    def s_elementwise(zh, qb, h, diag):
        buf = h % 2
        s_v = jnp.full((SUBLANES, tb), 0.5, jnp.float32)
        prev = None
        for v in range(SEG_LEN):
            t = jnp.tanh(zh[v * SUBLANES:(v + 1) * SUBLANES, :])
            if diag:
                t = jnp.where(SEG_LEN * sub + (SEG_LEN - 1 - v) < qcol, t, -1.0)
            u_v = s_v * t
            a_v = s_v + u_v
            s_v = s_v - u_v
            if v % 2 == 0:
                prev = a_v
            else:
                ap_scr[buf, (v - 1) * SUBLANES:(v + 1) * SUBLANES, :] = jnp.concatenate(
                    [prev, a_v], axis=0).astype(jnp.bfloat16)
        rem = s_v * (2.0 ** (1 - SEG_LEN))
        incl = rem
        for k in (1, 2, 4):
            shifted = pltpu.roll(incl, SUBLANES - k, 0)
            incl = incl * jnp.where(sub + k < SUBLANES, shifted, 1.0)
        excl = jnp.where(sub + 1 < SUBLANES, pltpu.roll(incl, SUBLANES - 1, 0), 1.0)
        carry = carry_scr[qb * ns + h]
        start = carry * excl
        carry_scr[qb * ns + h] = carry * jnp.broadcast_to(incl[0:1, :], carry.shape)
        start2 = jnp.concatenate([start, 0.5 * start], axis=0).astype(jnp.bfloat16)
        return jnp.concatenate(
            [ap_scr[buf, u * 2 * SUBLANES:(u + 1) * 2 * SUBLANES, :] * (start2 * (4.0 ** -u))
             for u in range(SEG_LEN // 2)], axis=0)

    def s_av(j, h, a):
        return lax.dot_general(svt_scr[j, h * HEAD_DIM:(h + 1) * HEAD_DIM, :], a, _NN,
                               preferred_element_type=jnp.float32)

    def tile_body(j, blocks, nxt, j_prev):
        if j_prev is not None:
            finish_deferred(j_prev)
        entries = []
        for qb, masked in blocks:
            for c in range(nd):
                entries.append(("s", qb, c, masked))
                entries.append(("d", qb, c, masked))
        sc = [None] * len(entries)
        sc[0] = pre_scr[...]
        issued = 1

        def issue(upto):
            nonlocal issued
            while issued <= upto:
                if issued < len(entries):
                    kind, qb, c, _ = entries[issued]
                    sc[issued] = s_scores(j, qb, c) if kind == "s" else d_scores(j, qb, c)
                elif issued == len(entries):
                    pre_scr[...] = s_scores(nxt[0], nxt[1], 0)
                issued += 1

        for k, (kind, qb, c, masked) in enumerate(entries):
            issue(k + 2)
            if kind == "s":
                for hh in range(2):
                    h = 2 * c + hh
                    a = s_elementwise(sc[k][:, hh * tb:(hh + 1) * tb], qb, h, masked)
                    rows = slice(h * HEAD_DIM, (h + 1) * HEAD_DIM)
                    if masked:
                        sacc_scr[qb, rows, :] = s_av(j, h, a)
                    else:
                        sacc_scr[qb, rows, :] += s_av(j, h, a)
            else:
                p, alpha = d_elementwise(sc[k], qb, c, masked)
                it = qb * nd + c
                if it == d_last:
                    p_scr[...] = p
                    if not masked:
                        alpha_scr[...] = alpha
                elif masked:
                    dacc_scr[it] = d_pv(j, c, p)
                else:
                    dacc_scr[it] = alpha * dacc_scr[it] + d_pv(j, c, p)
            sc[k] = None

    base = g * Q_GROUP
    pre_scr[...] = s_scores(base + Q_GROUP - 1, Q_GROUP - 1, 0)
    j_prev = None
    for d in range(Q_GROUP - 1, -1, -1):
        blocks = [(d, True)] + [(qb, False) for qb in range(d + 1, Q_GROUP)]
        nxt = (base + d - 1, d - 1) if d > 0 else (jnp.maximum(base - 1, 0), 0)
        tile_body(base + d, blocks, nxt, j_prev)
        j_prev = base + d

    def body(t, c):
        j = base - 1 - t
        tile_body(j, [(qb, False) for qb in range(Q_GROUP)], (jnp.maximum(j - 1, 0), 0), j + 1)
        return c

    if nblk > Q_GROUP:
        lax.fori_loop(0, base, body, 0)
    finish_deferred(0)

    g_col = jnp.broadcast_to(g_ref[...], (dv, dv)).T
    g_col = jnp.concatenate([g_col] * (tb // dv), axis=1)
    lam = (jnp.exp(jnp.sum(lq1_ref[...] * lk1_ref[...], axis=-1, keepdims=True))
           - jnp.exp(jnp.sum(lq2_ref[...] * lk2_ref[...], axis=-1, keepdims=True)) + lambda_init)
    for qb in range(Q_GROUP):
        for hd in range(nd):
            a1 = dacc_scr[qb * nd + hd, :, :tb]
            a2 = dacc_scr[qb * nd + hd, :, tb:]
            o_t = a1[:dv] / a1[dv:dv + 1] - lam * (a2[:dv] / a2[dv:dv + 1])
            ms = jnp.mean(o_t * o_t, axis=0, keepdims=True)
            o_t = o_t * lax.rsqrt(ms + NORM_EPS) * g_col * (1.0 - lambda_init)
            od_ref[0, qb * tb:(qb + 1) * tb, hd * LANES:(hd + 1) * LANES] = o_t.T.astype(od_ref.dtype)
        os_ref[0, qb * tb:(qb + 1) * tb, :] = sacc_scr[qb].T.astype(os_ref.dtype)


def _attn_call(proj4, lq1, lk1, lq2, lk2, subln_g, lambda_init, w_out, w_ff1, w_ff2):
    bsz, nblk, tb, _ = proj4.shape
    seq = nblk * tb
    w = DIFF_WIDTH
    assert SB_WIDTH == w and nblk % Q_GROUP == 0
    gsteps = nblk // Q_GROUP
    nsteps = bsz * gsteps

    def slab(arr):
        rows = arr.shape[0] // nsteps
        assert rows * nsteps == arr.shape[0] and rows % (2 * SUBLANES) == 0
        return pl.BlockSpec((rows, arr.shape[1]), lambda b, g: (b * gsteps + g, 0))

    wspecs = [slab(w_out), slab(w_ff1), slab(w_ff2)]
    wshapes = [jax.ShapeDtypeStruct(a.shape, jnp.bfloat16) for a in (w_out, w_ff1, w_ff2)]
    vrows = 2 * HEAD_DIM + ONES_ROWS
    nit = Q_GROUP * N_DIFF_HEADS
    vec = lambda a: a.reshape(1, HEAD_DIM)
    small = pl.BlockSpec((1, HEAD_DIM), lambda b, g: (0, 0))
    qspec = lambda col: pl.BlockSpec((1, Q_GROUP, tb, w), lambda b, g: (b, g, 0, col))
    kvspec = lambda col: pl.BlockSpec((1, nblk, tb, w), lambda b, g: (b, 0, 0, col))
    ospec = pl.BlockSpec((1, Q_GROUP * tb, w), lambda b, g: (b, g, 0))
    oshape = jax.ShapeDtypeStruct((bsz, seq, w), jnp.bfloat16)
    return pl.pallas_call(
        functools.partial(_attn_kernel, lambda_init=lambda_init),
        grid=(bsz, nblk // Q_GROUP),
        in_specs=[small, small, small, small,
                  pl.BlockSpec((1, 2 * HEAD_DIM), lambda b, g: (0, 0)),
                  qspec(0), kvspec(1), kvspec(2), qspec(3), kvspec(4), kvspec(5)] + wspecs,
        out_specs=[ospec, ospec] + wspecs,
        out_shape=[oshape, oshape] + wshapes,
        scratch_shapes=[pltpu.VMEM((nblk, N_DIFF_HEADS * vrows, tb), jnp.bfloat16),
                        pltpu.VMEM((nit, 2 * tb, LANES), jnp.bfloat16),
                        pltpu.VMEM((nit, vrows, 2 * tb), jnp.float32),
                        pltpu.VMEM((nit, 1, 2 * tb), jnp.float32),
                        pltpu.VMEM((2, tb, 2 * tb), jnp.float32),
                        pltpu.VMEM((tb, 2 * tb), jnp.bfloat16),
                        pltpu.VMEM((1, 2 * tb), jnp.float32),
                        pltpu.VMEM((nblk, tb, w), jnp.bfloat16),
                        pltpu.VMEM((nblk, w, tb), jnp.bfloat16),
                        pltpu.VMEM((Q_GROUP * N_PAIRS, 2 * tb, LANES), jnp.bfloat16),
                        pltpu.VMEM((Q_GROUP, w, tb), jnp.float32),
                        pltpu.VMEM((tb, 2 * tb), jnp.float32),
                        pltpu.VMEM((2, tb, tb), jnp.bfloat16),
                        pltpu.VMEM((Q_GROUP * N_SB_HEADS, SUBLANES, tb), jnp.float32)],
        compiler_params=pltpu.CompilerParams(
            dimension_semantics=("arbitrary", "arbitrary"), vmem_limit_bytes=VMEM_LIMIT),
        name="attn",
    )(vec(lq1), vec(lk1), vec(lq2), vec(lk2), subln_g.reshape(1, 2 * HEAD_DIM),
      proj4, proj4, proj4, proj4, proj4, proj4, w_out, w_ff1, w_ff2)


def _out_ffn_kernel(x_ref, od_ref, os_ref, mod_ref, wo_ref, gf_ref, w1_ref, w2_ref, gl_ref, o_ref):
    b = pl.program_id(0)
    g_m = mod_ref[2, pl.ds(b, 1), :]
    sh_f = mod_ref[3, pl.ds(b, 1), :]
    sc_f = mod_ref[4, pl.ds(b, 1), :]
    g_f = mod_ref[5, pl.ds(b, 1), :]
    d_ff = w1_ref.shape[1]
    fc = 1024
    tm = x_ref.shape[1]
    halves = [slice(k * (tm // 2), (k + 1) * (tm // 2)) for k in range(2)]
    x1, h, f = [None, None], [None, None], [None, None]
    for k, rows in enumerate(halves):
        mixed = (lax.dot_general(od_ref[0, rows, :], wo_ref[:DIFF_WIDTH, :], _NN, preferred_element_type=jnp.float32)
                 + lax.dot_general(os_ref[0, rows, :], wo_ref[DIFF_WIDTH:, :], _NN,
                                   preferred_element_type=jnp.float32))
        x1[k] = x_ref[0, rows, :] + g_m * mixed
    for k in range(2):
        h[k] = ((x1[k] * _rms_scale(x1[k]) * gf_ref[...]) * (1.0 + sc_f) + sh_f).astype(jnp.bfloat16)
    for c in range(d_ff // fc):
        for k in range(2):
            u = lax.dot_general(h[k], w1_ref[:, c * fc:(c + 1) * fc], _NN, preferred_element_type=jnp.float32)
            r = jnp.square(jnp.maximum(u, 0.0)).astype(jnp.bfloat16)
            part = lax.dot_general(r, w2_ref[c * fc:(c + 1) * fc, :], _NN, preferred_element_type=jnp.float32)
            f[k] = part if f[k] is None else f[k] + part
    for k, rows in enumerate(halves):
        x2 = x1[k] + g_f * f[k]
        o_ref[0, rows, :] = x2 * _rms_scale(x2) * gl_ref[...]


def _out_ffn_call(x, o_diff, o_sb, mod3, wo, gf, w1, w2, gl):
    bsz, seq, d = x.shape
    tm = TOKEN_TILE
    const = dict(pipeline_mode=pl.Buffered(1))
    tok = lambda w: pl.BlockSpec((1, tm, w), lambda b, t: (b, t, 0))
    return pl.pallas_call(
        _out_ffn_kernel,
        grid=(bsz, seq // tm),
        in_specs=[tok(d), tok(DIFF_WIDTH), tok(SB_WIDTH),
                  pl.BlockSpec((N_MOD, bsz, d), lambda b, t: (0, 0, 0)),
                  pl.BlockSpec(wo.shape, lambda b, t: (0, 0), **const),
                  pl.BlockSpec((1, d), lambda b, t: (0, 0)),
                  pl.BlockSpec(w1.shape, lambda b, t: (0, 0), **const),
                  pl.BlockSpec(w2.shape, lambda b, t: (0, 0), **const),
                  pl.BlockSpec((1, d), lambda b, t: (0, 0))],
        out_specs=tok(d),
        out_shape=jax.ShapeDtypeStruct((bsz, seq, d), jnp.float32),
        compiler_params=pltpu.CompilerParams(dimension_semantics=("arbitrary", "arbitrary"),
                                             vmem_limit_bytes=VMEM_LIMIT),
        name="out_ffn",
    )(x, o_diff, o_sb, mod3, wo, gf.reshape(1, d), w1, w2, gl.reshape(1, d))


def _rope_tables(seq_len):
    dim = HEAD_DIM
    inv = (1.0 / (np.float32(ROPE_THETA) ** (np.arange(0, dim, 2, dtype=np.float32) / np.float32(dim)))).astype(np.float32)
    ang = np.arange(seq_len, dtype=np.float32)[:, None] * inv[None, :]
    ang = np.concatenate([ang, ang], axis=-1)
    cos, sin = np.cos(ang).astype(np.float32), np.sin(ang).astype(np.float32)
    sign = np.where(np.arange(dim) < dim // 2, -1.0, 1.0).astype(np.float32)
    reps = LANES // dim
    return jnp.asarray(np.tile(cos, (1, reps))), jnp.asarray(np.tile(sin * sign[None, :], (1, reps)))


def kernel(x, c, ada_w, ada_b, mix_norm_g, w_in, lambda_q1, lambda_k1, lambda_q2, lambda_k2,
           diff_subln_g, w_out, ffn_norm_g, w_ff1, w_ff2, final_norm_g):
    bsz, seq, d = x.shape
    depth = ada_w.shape[0]
    assert depth == 1, "the fused final norm assumes a single layer"
    assert seq % TOKEN_TILE == 0 and seq % ATT_BLOCK == 0
    cos128, sin128 = _rope_tables(seq)
    layer = 0
    lambda_init = 0.8 - 0.6 * math.exp(-0.3 * layer)
    mod3 = _adaln_call(c, ada_w[layer], ada_b[layer])
    proj = _inproj_call(x, mod3, mix_norm_g[layer], w_in[layer], cos128, sin128)
    proj4 = proj.reshape(bsz, seq // ATT_BLOCK, ATT_BLOCK, proj.shape[-1])
    o_diff, o_sb, wo_bf, w1_bf, w2_bf = _attn_call(
        proj4, lambda_q1[layer], lambda_k1[layer], lambda_q2[layer], lambda_k2[layer], diff_subln_g[layer],
        lambda_init, w_out[layer], w_ff1[layer], w_ff2[layer])
    return _out_ffn_call(x, o_diff, o_sb, mod3, wo_bf, ffn_norm_g[layer], w1_bf, w2_bf, final_norm_g)
```

```python
import functools
import math

import jax
import jax.numpy as jnp
import numpy as np
from jax import lax
from jax.experimental import pallas as pl
from jax.experimental.pallas import tpu as pltpu

HEAD_DIM = 64
N_DIFF_HEADS = 4
N_SB_HEADS = 8
DIFF_WIDTH = N_DIFF_HEADS * 2 * HEAD_DIM
SB_WIDTH = N_SB_HEADS * HEAD_DIM
CHUNK = 64
ROPE_THETA = 10000.0
NORM_EPS = 1e-6
N_MOD = 6

LANES = 128
SUBLANES = 8
ATT_BLOCK = 256
SEG_LEN = ATT_BLOCK // SUBLANES
TOKEN_TILE = 512
N_PAIRS = N_SB_HEADS // 2
Q_GROUP = 4
ONES_ROWS = 16
LOG2E = 1.4426950408889634
VMEM_LIMIT = 56 * 1024 * 1024

_NT = (((1,), (1,)), ((), ()))
_NN = (((1,), (0,)), ((), ()))


def _rms_scale(xf):
    return lax.rsqrt(jnp.mean(xf * xf, axis=-1, keepdims=True) + NORM_EPS)


def _adaln_kernel(c_ref, w_ref, b_ref, o_ref):
    c = c_ref[...]
    ca = c / (1.0 + jnp.exp(-c))
    o_ref[0] = lax.dot_general(ca.astype(jnp.bfloat16), w_ref[...].astype(jnp.bfloat16), _NN,
                               preferred_element_type=jnp.float32) + b_ref[...]


def _adaln_call(c, w, b):
    bsz, d = c.shape
    n = w.shape[1]
    tn = d
    return pl.pallas_call(
        _adaln_kernel,
        grid=(n // tn,),
        in_specs=[pl.BlockSpec((bsz, d), lambda j: (0, 0)),
                  pl.BlockSpec((d, tn), lambda j: (0, j)),
                  pl.BlockSpec((1, tn), lambda j: (0, j))],
        out_specs=pl.BlockSpec((1, bsz, tn), lambda j: (j, 0, 0)),
        out_shape=jax.ShapeDtypeStruct((n // tn, bsz, tn), jnp.float32),
        compiler_params=pltpu.CompilerParams(dimension_semantics=("arbitrary",),
                                             vmem_limit_bytes=VMEM_LIMIT),
        name="adaln",
    )(c, w, b.reshape(1, n))


def _inproj_kernel(x_ref, mod_ref, g_ref, w_ref, cos_ref, sin_ref, o_ref, wb_scr):
    b = pl.program_id(0)
    gw = DIFF_WIDTH

    @pl.when((b == 0) & (pl.program_id(1) == 0))
    def _():
        for n in range(w_ref.shape[1] // gw):
            wb_scr[:, n * gw:(n + 1) * gw] = w_ref[:, n * gw:(n + 1) * gw].astype(wb_scr.dtype)

    xf = x_ref[0]
    sh = mod_ref[0, pl.ds(b, 1), :]
    sc = mod_ref[1, pl.ds(b, 1), :]
    h = (xf * _rms_scale(xf) * g_ref[...]) * (1.0 + sc) + sh
    hb = h.astype(jnp.bfloat16)
    cos = cos_ref[...]
    sin = sin_ref[...]
    lane = lax.broadcasted_iota(jnp.int32, cos.shape, 1)
    first_half = (lane % HEAD_DIM) < (HEAD_DIM // 2)
    scale = HEAD_DIM ** -0.5
    for n in range(6):
        acc = lax.dot_general(hb, wb_scr[:, n * gw:(n + 1) * gw], _NN,
                              preferred_element_type=jnp.float32)
        if n < 2:
            for j in range(gw // LANES):
                xb = acc[:, j * LANES:(j + 1) * LANES]
                rot = jnp.where(first_half, pltpu.roll(xb, LANES - HEAD_DIM // 2, 1),
                                pltpu.roll(xb, HEAD_DIM // 2, 1))
                yb = xb * cos + rot * sin
                if n == 0:
                    yb = yb * (scale * LOG2E)
                o_ref[0, :, n * gw + j * LANES:n * gw + (j + 1) * LANES] = yb.astype(o_ref.dtype)
        else:
            if n == 3:
                acc = acc * (0.5 * scale)
            o_ref[0, :, n * gw:(n + 1) * gw] = acc.astype(o_ref.dtype)


def _inproj_call(x, mod3, g, w, cos128, sin128):
    bsz, seq, d = x.shape
    n = w.shape[1]
    tm = TOKEN_TILE
    const = dict(pipeline_mode=pl.Buffered(1))
    return pl.pallas_call(
        _inproj_kernel,
        grid=(bsz, seq // tm),
        in_specs=[pl.BlockSpec((1, tm, d), lambda b, t: (b, t, 0)),
                  pl.BlockSpec((N_MOD, bsz, d), lambda b, t: (0, 0, 0)),
                  pl.BlockSpec((1, d), lambda b, t: (0, 0)),
                  pl.BlockSpec((d, n), lambda b, t: (0, 0), **const),
                  pl.BlockSpec((tm, LANES), lambda b, t: (t, 0)),
                  pl.BlockSpec((tm, LANES), lambda b, t: (t, 0))],
        out_specs=pl.BlockSpec((1, tm, n), lambda b, t: (b, t, 0)),
        out_shape=jax.ShapeDtypeStruct((bsz, seq, n), jnp.bfloat16),
        scratch_shapes=[pltpu.VMEM((d, n), jnp.bfloat16)],
        compiler_params=pltpu.CompilerParams(dimension_semantics=("arbitrary", "arbitrary"),
                                             vmem_limit_bytes=VMEM_LIMIT),
        name="inproj",
    )(x, mod3, g.reshape(1, d), w, cos128, sin128)


def _attn_kernel(lq1_ref, lk1_ref, lq2_ref, lk2_ref, g_ref, dq_ref, dk_ref, dv_ref, sq_ref, sk_ref, sv_ref,
                 wo_ref, w1_ref, w2_ref, od_ref, os_ref, wo_bf_ref, w1_bf_ref, w2_bf_ref,
                 dvt_scr, dqm_scr, dacc_scr, m_scr, s_scr, p_scr, alpha_scr,
                 kp_scr, svt_scr, sqm_scr, sacc_scr, pre_scr, ap_scr, carry_scr, *, lambda_init):
    g = pl.program_id(1)
    nblk = dk_ref.shape[1]
    tb = ATT_BLOCK
    tw = 2 * tb
    dv = 2 * HEAD_DIM
    vrows = dv + ONES_ROWS
    nd = N_DIFF_HEADS
    ns = N_SB_HEADS
    d_last = Q_GROUP * nd - 1

    @pl.when(g == 0)
    def _():
        for blk in range(nblk):
            vt = dv_ref[0, blk].astype(jnp.float32).T.astype(jnp.bfloat16)
            for hd in range(nd):
                dvt_scr[blk, hd * vrows:hd * vrows + dv, :] = vt[hd * dv:(hd + 1) * dv, :]
                dvt_scr[blk, hd * vrows + dv:(hd + 1) * vrows, :] = jnp.ones((ONES_ROWS, tb), jnp.bfloat16)
        row = lax.broadcasted_iota(jnp.int32, (tb, tb), 0)
        col = lax.broadcasted_iota(jnp.int32, (tb, tb), 1)
        pos = SEG_LEN * (row % SUBLANES) + (SEG_LEN - 1 - row // SUBLANES)
        perm = jnp.where(col == pos, 1.0, 0.0).astype(jnp.bfloat16)
        for blk in range(nblk):
            kp = lax.dot_general(perm, sk_ref[0, blk], _NN, preferred_element_type=jnp.float32)
            kp_scr[blk] = kp.astype(jnp.bfloat16)
            vp = lax.dot_general(perm, sv_ref[0, blk], _NN, preferred_element_type=jnp.float32)
            svt_scr[blk] = vp.T.astype(jnp.bfloat16)

    wo_bf_ref[...] = wo_ref[...].astype(wo_bf_ref.dtype)
    w1_bf_ref[...] = w1_ref[...].astype(w1_bf_ref.dtype)
    w2_bf_ref[...] = w2_ref[...].astype(w2_bf_ref.dtype)

    lane = lax.broadcasted_iota(jnp.int32, (tb, LANES), 1)
    for qb in range(Q_GROUP):
        for c in range(nd):
            for ref, scr in ((dq_ref, dqm_scr), (sq_ref, sqm_scr)):
                qp = ref[0, qb, :, c * LANES:(c + 1) * LANES]
                zero = jnp.zeros_like(qp)
                scr[qb * nd + c, :tb, :] = jnp.where(lane < HEAD_DIM, qp, zero)
                scr[qb * nd + c, tb:, :] = jnp.where(lane >= HEAD_DIM, qp, zero)
        for h in range(ns):
            carry_scr[qb * ns + h] = jnp.ones((SUBLANES, tb), jnp.float32)
    dacc_scr[d_last] = jnp.zeros((vrows, tw), jnp.float32)
    alpha_scr[...] = jnp.zeros((1, tw), jnp.float32)

    sub = lax.broadcasted_iota(jnp.int32, (SUBLANES, tb), 0)
    qcol = lax.broadcasted_iota(jnp.int32, (SUBLANES, tb), 1)
    qcol2 = lax.broadcasted_iota(jnp.int32, (SUBLANES, tw), 1) % tb

    def d_scores(j, qb, hd):
        return lax.dot_general(dk_ref[0, j, :, hd * LANES:(hd + 1) * LANES], dqm_scr[qb * nd + hd], _NT,
                               preferred_element_type=jnp.float32)

    def d_elementwise(s_t, qb, hd, first):
        buf = hd % 2
        it = qb * nd + hd
        mx = None
        for v in range(tb // SUBLANES):
            sv = s_t[v * SUBLANES:(v + 1) * SUBLANES, :]
            if first:
                sv = jnp.where(qcol2 >= (v * SUBLANES // CHUNK) * CHUNK, sv, -jnp.inf)
            mx = sv if mx is None else jnp.maximum(mx, sv)
            s_scr[buf, v * SUBLANES:(v + 1) * SUBLANES, :] = sv
        tmax = jnp.max(mx, axis=0, keepdims=True)
        if first:
            m_new = tmax
            alpha = None
        else:
            m_old = m_scr[it]
            m_new = jnp.maximum(m_old, tmax)
            alpha = jnp.exp2(m_old - m_new)
        m_scr[it] = m_new
        return jnp.exp2(s_scr[buf] - m_new).astype(jnp.bfloat16), alpha

    def d_pv(j, hd, p):
        return lax.dot_general(dvt_scr[j, hd * vrows:(hd + 1) * vrows, :], p, _NN,
                               preferred_element_type=jnp.float32)

    def finish_deferred(j_prev):
        dacc_scr[d_last] = alpha_scr[...] * dacc_scr[d_last] + d_pv(j_prev, nd - 1, p_scr[...])

    def s_scores(j, qb, pair):
        return lax.dot_general(kp_scr[j, :, pair * LANES:(pair + 1) * LANES], sqm_scr[qb * N_PAIRS + pair], _NT,
                               preferred_element_type=jnp.float32)

    def s_elementwise(zh, qb, h, diag):
        buf = h % 2
        rem = jnp.ones((SUBLANES, tb), jnp.float32)
        prev = None
        for v in range(SEG_LEN):
            beta = 0.5 * jnp.tanh(zh[v * SUBLANES:(v + 1) * SUBLANES, :]) + 0.5
            if diag:
                beta = jnp.where(SEG_LEN * sub + (SEG_LEN - 1 - v) < qcol, beta, 0.0)
            a_v = beta * rem
            rem = rem - a_v
            if v % 2 == 0:
                prev = a_v
            else:
                ap_scr[buf, (v - 1) * SUBLANES:(v + 1) * SUBLANES, :] = jnp.concatenate(
                    [prev, a_v], axis=0).astype(jnp.bfloat16)
        incl = rem
        for k in (1, 2, 4):
            shifted = pltpu.roll(incl, SUBLANES - k, 0)
            incl = incl * jnp.where(sub + k < SUBLANES, shifted, 1.0)
        excl = jnp.where(sub + 1 < SUBLANES, pltpu.roll(incl, SUBLANES - 1, 0), 1.0)
        carry = carry_scr[qb * ns + h]
        start = carry * excl
        carry_scr[qb * ns + h] = carry * jnp.broadcast_to(incl[0:1, :], carry.shape)
        start2 = jnp.concatenate([start, start], axis=0).astype(jnp.bfloat16)
        return jnp.concatenate(
            [ap_scr[buf, u * 2 * SUBLANES:(u + 1) * 2 * SUBLANES, :] * start2 for u in range(SEG_LEN // 2)],
            axis=0)

    def s_av(j, h, a):
        return lax.dot_general(svt_scr[j, h * HEAD_DIM:(h + 1) * HEAD_DIM, :], a, _NN,
                               preferred_element_type=jnp.float32)

    def tile_body(j, blocks, nxt, j_prev):
        if j_prev is not None:
            finish_deferred(j_prev)
        entries = []
        for qb, masked in blocks:
            for c in range(nd):
                entries.append(("s", qb, c, masked))
                entries.append(("d", qb, c, masked))
        sc = [None] * len(entries)
        sc[0] = pre_scr[...]
        issued = 1

        def issue(upto):
            nonlocal issued
            while issued <= upto:
                if issued < len(entries):
                    kind, qb, c, _ = entries[issued]
                    sc[issued] = s_scores(j, qb, c) if kind == "s" else d_scores(j, qb, c)
                elif issued == len(entries):
                    pre_scr[...] = s_scores(nxt[0], nxt[1], 0)
                issued += 1

        for k, (kind, qb, c, masked) in enumerate(entries):
            issue(k + 2)
            if kind == "s":
                for hh in range(2):
                    h = 2 * c + hh
                    a = s_elementwise(sc[k][:, hh * tb:(hh + 1) * tb], qb, h, masked)
                    rows = slice(h * HEAD_DIM, (h + 1) * HEAD_DIM)
                    if masked:
                        sacc_scr[qb, rows, :] = s_av(j, h, a)
                    else:
                        sacc_scr[qb, rows, :] += s_av(j, h, a)
            else:
                p, alpha = d_elementwise(sc[k], qb, c, masked)
                it = qb * nd + c
                if it == d_last:
                    p_scr[...] = p
                    if not masked:
                        alpha_scr[...] = alpha
                elif masked:
                    dacc_scr[it] = d_pv(j, c, p)
                else:
                    dacc_scr[it] = alpha * dacc_scr[it] + d_pv(j, c, p)
            sc[k] = None

    base = g * Q_GROUP
    pre_scr[...] = s_scores(base + Q_GROUP - 1, Q_GROUP - 1, 0)
    j_prev = None
    for d in range(Q_GROUP - 1, -1, -1):
        blocks = [(d, True)] + [(qb, False) for qb in range(d + 1, Q_GROUP)]
        nxt = (base + d - 1, d - 1) if d > 0 else (jnp.maximum(base - 1, 0), 0)
        tile_body(base + d, blocks, nxt, j_prev)
        j_prev = base + d

    def body(t, c):
        j = base - 1 - t
        tile_body(j, [(qb, False) for qb in range(Q_GROUP)], (jnp.maximum(j - 1, 0), 0), j + 1)
        return c

    if nblk > Q_GROUP:
        lax.fori_loop(0, base, body, 0)
    finish_deferred(0)

    g_col = jnp.broadcast_to(g_ref[...], (dv, dv)).T
    g_col = jnp.concatenate([g_col] * (tb // dv), axis=1)
    lam = (jnp.exp(jnp.sum(lq1_ref[...] * lk1_ref[...], axis=-1, keepdims=True))
           - jnp.exp(jnp.sum(lq2_ref[...] * lk2_ref[...], axis=-1, keepdims=True)) + lambda_init)
    for qb in range(Q_GROUP):
        for hd in range(nd):
            a1 = dacc_scr[qb * nd + hd, :, :tb]
            a2 = dacc_scr[qb * nd + hd, :, tb:]
            o_t = a1[:dv] / a1[dv:dv + 1] - lam * (a2[:dv] / a2[dv:dv + 1])
            ms = jnp.mean(o_t * o_t, axis=0, keepdims=True)
            o_t = o_t * lax.rsqrt(ms + NORM_EPS) * g_col * (1.0 - lambda_init)
            od_ref[0, qb * tb:(qb + 1) * tb, hd * LANES:(hd + 1) * LANES] = o_t.T.astype(od_ref.dtype)
        os_ref[0, qb * tb:(qb + 1) * tb, :] = sacc_scr[qb].T.astype(os_ref.dtype)


def _attn_call(proj4, lq1, lk1, lq2, lk2, subln_g, lambda_init, w_out, w_ff1, w_ff2):
    bsz, nblk, tb, _ = proj4.shape
    seq = nblk * tb
    w = DIFF_WIDTH
    assert SB_WIDTH == w and nblk % Q_GROUP == 0
    gsteps = nblk // Q_GROUP
    nsteps = bsz * gsteps

    def slab(arr):
        rows = arr.shape[0] // nsteps
        assert rows * nsteps == arr.shape[0] and rows % (2 * SUBLANES) == 0
        return pl.BlockSpec((rows, arr.shape[1]), lambda b, g: (b * gsteps + g, 0))

    wspecs = [slab(w_out), slab(w_ff1), slab(w_ff2)]
    wshapes = [jax.ShapeDtypeStruct(a.shape, jnp.bfloat16) for a in (w_out, w_ff1, w_ff2)]
    vrows = 2 * HEAD_DIM + ONES_ROWS
    nit = Q_GROUP * N_DIFF_HEADS
    vec = lambda a: a.reshape(1, HEAD_DIM)
    small = pl.BlockSpec((1, HEAD_DIM), lambda b, g: (0, 0))
    qspec = lambda col: pl.BlockSpec((1, Q_GROUP, tb, w), lambda b, g: (b, g, 0, col))
    kvspec = lambda col: pl.BlockSpec((1, nblk, tb, w), lambda b, g: (b, 0, 0, col))
    ospec = pl.BlockSpec((1, Q_GROUP * tb, w), lambda b, g: (b, g, 0))
    oshape = jax.ShapeDtypeStruct((bsz, seq, w), jnp.bfloat16)
    return pl.pallas_call(
        functools.partial(_attn_kernel, lambda_init=lambda_init),
        grid=(bsz, nblk // Q_GROUP),
        in_specs=[small, small, small, small,
                  pl.BlockSpec((1, 2 * HEAD_DIM), lambda b, g: (0, 0)),
                  qspec(0), kvspec(1), kvspec(2), qspec(3), kvspec(4), kvspec(5)] + wspecs,
        out_specs=[ospec, ospec] + wspecs,
        out_shape=[oshape, oshape] + wshapes,
        scratch_shapes=[pltpu.VMEM((nblk, N_DIFF_HEADS * vrows, tb), jnp.bfloat16),
                        pltpu.VMEM((nit, 2 * tb, LANES), jnp.bfloat16),
                        pltpu.VMEM((nit, vrows, 2 * tb), jnp.float32),
                        pltpu.VMEM((nit, 1, 2 * tb), jnp.float32),
                        pltpu.VMEM((2, tb, 2 * tb), jnp.float32),
                        pltpu.VMEM((tb, 2 * tb), jnp.bfloat16),
                        pltpu.VMEM((1, 2 * tb), jnp.float32),
                        pltpu.VMEM((nblk, tb, w), jnp.bfloat16),
                        pltpu.VMEM((nblk, w, tb), jnp.bfloat16),
                        pltpu.VMEM((Q_GROUP * N_PAIRS, 2 * tb, LANES), jnp.bfloat16),
                        pltpu.VMEM((Q_GROUP, w, tb), jnp.float32),
                        pltpu.VMEM((tb, 2 * tb), jnp.float32),
                        pltpu.VMEM((2, tb, tb), jnp.bfloat16),
                        pltpu.VMEM((Q_GROUP * N_SB_HEADS, SUBLANES, tb), jnp.float32)],
        compiler_params=pltpu.CompilerParams(
            dimension_semantics=("arbitrary", "arbitrary"), vmem_limit_bytes=VMEM_LIMIT),
        name="attn",
    )(vec(lq1), vec(lk1), vec(lq2), vec(lk2), subln_g.reshape(1, 2 * HEAD_DIM),
      proj4, proj4, proj4, proj4, proj4, proj4, w_out, w_ff1, w_ff2)


def _out_ffn_kernel(x_ref, od_ref, os_ref, mod_ref, wo_ref, gf_ref, w1_ref, w2_ref, gl_ref, o_ref):
    b = pl.program_id(0)
    g_m = mod_ref[2, pl.ds(b, 1), :]
    sh_f = mod_ref[3, pl.ds(b, 1), :]
    sc_f = mod_ref[4, pl.ds(b, 1), :]
    g_f = mod_ref[5, pl.ds(b, 1), :]
    d_ff = w1_ref.shape[1]
    fc = 1024
    tm = x_ref.shape[1]
    halves = [slice(k * (tm // 2), (k + 1) * (tm // 2)) for k in range(2)]
    x1, h, f = [None, None], [None, None], [None, None]
    for k, rows in enumerate(halves):
        mixed = (lax.dot_general(od_ref[0, rows, :], wo_ref[:DIFF_WIDTH, :], _NN, preferred_element_type=jnp.float32)
                 + lax.dot_general(os_ref[0, rows, :], wo_ref[DIFF_WIDTH:, :], _NN,
                                   preferred_element_type=jnp.float32))
        x1[k] = x_ref[0, rows, :] + g_m * mixed
    for k in range(2):
        h[k] = ((x1[k] * _rms_scale(x1[k]) * gf_ref[...]) * (1.0 + sc_f) + sh_f).astype(jnp.bfloat16)
    for c in range(d_ff // fc):
        for k in range(2):
            u = lax.dot_general(h[k], w1_ref[:, c * fc:(c + 1) * fc], _NN, preferred_element_type=jnp.float32)
            r = jnp.square(jnp.maximum(u, 0.0)).astype(jnp.bfloat16)
            part = lax.dot_general(r, w2_ref[c * fc:(c + 1) * fc, :], _NN, preferred_element_type=jnp.float32)
            f[k] = part if f[k] is None else f[k] + part
    for k, rows in enumerate(halves):
        x2 = x1[k] + g_f * f[k]
        o_ref[0, rows, :] = x2 * _rms_scale(x2) * gl_ref[...]


def _out_ffn_call(x, o_diff, o_sb, mod3, wo, gf, w1, w2, gl):
    bsz, seq, d = x.shape
    tm = 2 * TOKEN_TILE
    const = dict(pipeline_mode=pl.Buffered(1))
    tok = lambda w: pl.BlockSpec((1, tm, w), lambda b, t: (b, t, 0))
    return pl.pallas_call(
        _out_ffn_kernel,
        grid=(bsz, seq // tm),
        in_specs=[tok(d), tok(DIFF_WIDTH), tok(SB_WIDTH),
                  pl.BlockSpec((N_MOD, bsz, d), lambda b, t: (0, 0, 0)),
                  pl.BlockSpec(wo.shape, lambda b, t: (0, 0), **const),
                  pl.BlockSpec((1, d), lambda b, t: (0, 0)),
                  pl.BlockSpec(w1.shape, lambda b, t: (0, 0), **const),
                  pl.BlockSpec(w2.shape, lambda b, t: (0, 0), **const),
                  pl.BlockSpec((1, d), lambda b, t: (0, 0))],
        out_specs=tok(d),
        out_shape=jax.ShapeDtypeStruct((bsz, seq, d), jnp.float32),
        compiler_params=pltpu.CompilerParams(dimension_semantics=("arbitrary", "arbitrary"),
                                             vmem_limit_bytes=VMEM_LIMIT),
        name="out_ffn",
    )(x, o_diff, o_sb, mod3, wo, gf.reshape(1, d), w1, w2, gl.reshape(1, d))


def _rope_tables(seq_len):
    dim = HEAD_DIM
    inv = (1.0 / (np.float32(ROPE_THETA) ** (np.arange(0, dim, 2, dtype=np.float32) / np.float32(dim)))).astype(np.float32)
    ang = np.arange(seq_len, dtype=np.float32)[:, None] * inv[None, :]
    ang = np.concatenate([ang, ang], axis=-1)
    cos, sin = np.cos(ang).astype(np.float32), np.sin(ang).astype(np.float32)
    sign = np.where(np.arange(dim) < dim // 2, -1.0, 1.0).astype(np.float32)
    reps = LANES // dim
    return jnp.asarray(np.tile(cos, (1, reps))), jnp.asarray(np.tile(sin * sign[None, :], (1, reps)))


def kernel(x, c, ada_w, ada_b, mix_norm_g, w_in, lambda_q1, lambda_k1, lambda_q2, lambda_k2,
           diff_subln_g, w_out, ffn_norm_g, w_ff1, w_ff2, final_norm_g):
    bsz, seq, d = x.shape
    depth = ada_w.shape[0]
    assert depth == 1, "the fused final norm assumes a single layer"
    assert seq % TOKEN_TILE == 0 and seq % ATT_BLOCK == 0
    cos128, sin128 = _rope_tables(seq)
    layer = 0
    lambda_init = 0.8 - 0.6 * math.exp(-0.3 * layer)
    mod3 = _adaln_call(c, ada_w[layer], ada_b[layer])
    proj = _inproj_call(x, mod3, mix_norm_g[layer], w_in[layer], cos128, sin128)
    proj4 = proj.reshape(bsz, seq // ATT_BLOCK, ATT_BLOCK, proj.shape[-1])
    o_diff, o_sb, wo_bf, w1_bf, w2_bf = _attn_call(
        proj4, lambda_q1[layer], lambda_k1[layer], lambda_q2[layer], lambda_k2[layer], diff_subln_g[layer],
        lambda_init, w_out[layer], w_ff1[layer], w_ff2[layer])
    return _out_ffn_call(x, o_diff, o_sb, mod3, wo_bf, ffn_norm_g[layer], w1_bf, w2_bf, final_norm_g)
```
